```python
import math
import jax, jax.numpy as jnp
from jax import lax
import numpy as np

D_MODEL = 2048
BATCH = 4
SEQ = 4096
DEPTH = 1

POOL_WIDTH = D_MODEL // 2
POOL_GROUPS = 4
POOL_WINDOWS = (2, 4, 8, 16)
POOL_GROUP_DIM = POOL_WIDTH // POOL_GROUPS
GLA_V_WIDTH = D_MODEL - POOL_WIDTH
GLA_HEADS = 4
GLA_HEAD_V = GLA_V_WIDTH // GLA_HEADS
GLA_K_WIDTH = GLA_V_WIDTH // 2
GLA_HEAD_K = GLA_K_WIDTH // GLA_HEADS
GLA_GATE_RANK = 16
GLA_GATE_TAU = 16.0
GLA_CHUNK = 64
GLA_NORM_EPS = 1e-6
SPLITS = (POOL_WIDTH,
          POOL_WIDTH + GLA_K_WIDTH,
          POOL_WIDTH + 2 * GLA_K_WIDTH,
          POOL_WIDTH + 2 * GLA_K_WIDTH + GLA_V_WIDTH,
          POOL_WIDTH + 2 * GLA_K_WIDTH + GLA_V_WIDTH + GLA_GATE_RANK)
IN_WIDTH = SPLITS[-1] + GLA_V_WIDTH
N_EXPERTS = 32
TOP_K = 4
D_FF = D_MODEL
SWIGLU_ALPHA = 1.702
SWIGLU_LIMIT = 7.0
MOE_BLOCK = 128
DEEPNORM_ALPHA = (2.0 * DEPTH) ** 0.25
DEEPNORM_BETA = (8.0 * DEPTH) ** -0.25
LN_EPS = 1e-5

kernel_name = "hymba_pool_gla_moe_adaln_deepnorm"


def _layer_norm(x):
    xf = x.astype(jnp.float32)
    mu = jnp.mean(xf, axis=-1, keepdims=True)
    var = jnp.mean(jnp.square(xf - mu), axis=-1, keepdims=True)
    return (xf - mu) * lax.rsqrt(var + LN_EPS)


def _post_ln(y, g, b, dtype):
    return (_layer_norm(y) * g + b).astype(dtype)


def _modulate(x, shift, scale):
    return (_layer_norm(x) * (1.0 + scale) + shift).astype(x.dtype)


def _pool_mixer(u, w_pool, b_pool, pool_scale):
    bsz, s, _ = u.shape
    uf = u.astype(jnp.float32).reshape(bsz, s, POOL_GROUPS, POOL_GROUP_DIM)
    cs = jnp.cumsum(uf, axis=1)
    pos = jnp.arange(1, s + 1)
    outs = []
    for g, w in enumerate(POOL_WINDOWS):
        csg = cs[:, :, g]
        lagged = jnp.pad(csg[:, : s - w], ((0, 0), (w, 0), (0, 0)))
        count = jnp.minimum(pos, w).astype(jnp.float32)[None, :, None]
        outs.append((csg - lagged) / count - uf[:, :, g])
    pooled = jnp.stack(outs, axis=2)
    mixed = jnp.einsum('bsgc,gcd->bsgd', pooled, w_pool.astype(jnp.float32))
    mixed = mixed.reshape(bsz, s, POOL_WIDTH) + b_pool
    return (mixed * pool_scale).astype(u.dtype)


def _gla_mixer(q, k, v, gk_lowrank, g, w_gk, b_gk, norm_w):
    bsz, s, _ = q.shape
    n_chunks = s // GLA_CHUNK
    f32 = jnp.float32

    def to_chunks(a, dh):
        a = a.astype(f32).reshape(bsz, n_chunks, GLA_CHUNK, GLA_HEADS, dh)
        return a.transpose(1, 0, 3, 2, 4)

    log_alpha = jax.nn.log_sigmoid(gk_lowrank.astype(f32) @ w_gk.astype(f32) + b_gk) / GLA_GATE_TAU
    qc = to_chunks(q, GLA_HEAD_K) * (GLA_HEAD_K ** -0.5)
    kc = to_chunks(k, GLA_HEAD_K)
    vc = to_chunks(v, GLA_HEAD_V)
    ac = to_chunks(log_alpha, GLA_HEAD_K)
    causal = jnp.tril(jnp.ones((GLA_CHUNK, GLA_CHUNK), dtype=bool))[None, None, :, :, None]

    def chunk_step(state, inp):
        qb, kb, vb, ab = inp
        bcum = jnp.cumsum(ab, axis=2)
        blast = bcum[:, :, -1:, :]
        o_inter = jnp.einsum('bhtd,bhdv->bhtv', qb * jnp.exp(bcum), state)
        rel = jnp.where(causal, bcum[:, :, :, None, :] - bcum[:, :, None, :, :], -jnp.inf)
        scores = jnp.einsum('bhtd,bhsd,bhtsd->bhts', qb, kb, jnp.exp(rel))
        o_intra = jnp.einsum('bhts,bhsv->bhtv', scores, vb)
        new_state = jnp.exp(blast[:, :, 0, :])[..., None] * state + jnp.einsum(
            'bhsd,bhsv->bhdv', kb * jnp.exp(blast - bcum), vb)
        return new_state, o_inter + o_intra

    state0 = jnp.zeros((bsz, GLA_HEADS, GLA_HEAD_K, GLA_HEAD_V), f32)
    _, o = lax.scan(chunk_step, state0, (qc, kc, vc, ac))
    o = o.transpose(1, 0, 3, 2, 4).reshape(bsz, s, GLA_HEADS, GLA_HEAD_V)
    o = o * lax.rsqrt(jnp.mean(o * o, axis=-1, keepdims=True) + GLA_NORM_EPS) * norm_w
    gate = jax.nn.silu(g.astype(f32)).reshape(bsz, s, GLA_HEADS, GLA_HEAD_V)
    return (o * gate).reshape(bsz, s, GLA_V_WIDTH).astype(q.dtype)


def _mixer(h, w_in, w_gk, b_gk, w_pool, b_pool, pool_scale, gla_norm_w, w_out):
    proj = h @ w_in
    u, q, k, v, gk_lr, g = jnp.split(proj, list(SPLITS), axis=-1)
    y_pool = _pool_mixer(u, w_pool, b_pool, pool_scale)
    y_gla = _gla_mixer(q, k, v, gk_lr, g, w_gk, b_gk, gla_norm_w)
    return jnp.concatenate([y_pool, y_gla], axis=-1) @ w_out


def _moe(h, w_router, b_router, w_gate, b_gate, w_up, b_up, w_down, b_down):
    bsz, s, d = h.shape
    n_tok = bsz * s
    xf = h.reshape(n_tok, d)
    logits = (xf @ w_router + b_router).astype(jnp.float32)
    top_vals, top_idx = lax.top_k(logits, TOP_K)
    top_w = jax.nn.softmax(top_vals, axis=-1)
    n_assign = n_tok * TOP_K
    flat_e = top_idx.reshape(-1)
    flat_tok = jnp.repeat(jnp.arange(n_tok, dtype=jnp.int32), TOP_K)
    flat_w = top_w.reshape(-1)
    order = jnp.argsort(flat_e)
    sorted_e = flat_e[order]
    sorted_tok = flat_tok[order]
    sorted_w = flat_w[order]
    counts = jnp.bincount(flat_e, length=N_EXPERTS)
    padded = (counts + MOE_BLOCK - 1) // MOE_BLOCK * MOE_BLOCK
    group_start = jnp.cumsum(counts) - counts
    pad_end = jnp.cumsum(padded)
    pad_start = pad_end - padded
    rank = jnp.arange(n_assign, dtype=jnp.int32) - group_start[sorted_e]
    dest = pad_start[sorted_e] + rank
    n_slots = (n_assign + MOE_BLOCK - 1) // MOE_BLOCK * MOE_BLOCK + N_EXPERTS * MOE_BLOCK
    n_blocks = n_slots // MOE_BLOCK
    slot_tok = jnp.zeros((n_slots,), jnp.int32).at[dest].set(sorted_tok)
    slot_w = jnp.zeros((n_slots,), jnp.float32).at[dest].set(sorted_w)
    block_e = jnp.minimum(
        jnp.searchsorted(pad_end, jnp.arange(n_blocks) * MOE_BLOCK, side='right'), N_EXPERTS - 1)

    def expert_block(args):
        tok, e = args
        xb = xf[tok]
        gate = jnp.minimum(xb @ w_gate[e] + b_gate[e], SWIGLU_LIMIT)
        up = jnp.clip(xb @ w_up[e] + b_up[e], -SWIGLU_LIMIT, SWIGLU_LIMIT)
        act = gate * jax.nn.sigmoid(SWIGLU_ALPHA * gate) * (up + 1.0)
        return act @ w_down[e] + b_down[e]

    y = lax.map(expert_block, (slot_tok.reshape(n_blocks, MOE_BLOCK), block_e))
    y = y.reshape(n_slots, d) * slot_w[:, None].astype(y.dtype)
    out = jnp.zeros((n_tok, d), y.dtype).at[slot_tok].add(y)
    return out.reshape(bsz, s, d)


def setup_inputs(seed: int = 0) -> dict:
    key = jax.random.key(seed)
    ks = jax.random.split(key, 24)
    L, D, E, F = DEPTH, D_MODEL, N_EXPERTS, D_FF

    def nrm(k, shape, scale):
        return jax.random.normal(k, shape, jnp.float32) * scale

    col_scale = jnp.concatenate([
        jnp.ones((SPLITS[2],), jnp.float32),
        jnp.full((GLA_V_WIDTH,), DEEPNORM_BETA, jnp.float32),
        jnp.ones((IN_WIDTH - SPLITS[3],), jnp.float32)])
    return {
        "x": nrm(ks[0], (BATCH, SEQ, D), 1.0),
        "c": nrm(ks[1], (BATCH, D), 1.0),
        "w_ada": nrm(ks[2], (L, D, 6 * D), 0.5 * D ** -0.5),
        "b_ada": nrm(ks[3], (L, 6 * D), 0.02),
        "w_in": nrm(ks[4], (L, D, IN_WIDTH), D ** -0.5) * col_scale,
        "w_gk": nrm(ks[5], (L, GLA_GATE_RANK, GLA_K_WIDTH), GLA_GATE_RANK ** -0.5),
        "b_gk": nrm(ks[6], (L, GLA_K_WIDTH), 0.1),
        "w_pool": nrm(ks[7], (L, POOL_GROUPS, POOL_GROUP_DIM, POOL_GROUP_DIM), POOL_GROUP_DIM ** -0.5),
        "b_pool": nrm(ks[8], (L, POOL_WIDTH), 0.02),
        "pool_scale": 1.0 + nrm(ks[9], (L, POOL_WIDTH), 0.05),
        "gla_norm_w": 1.0 + nrm(ks[10], (L, GLA_HEAD_V), 0.05),
        "w_out": nrm(ks[11], (L, D, D), DEEPNORM_BETA * D ** -0.5),
        "ln1_g": 1.0 + nrm(ks[12], (L, D), 0.05),
        "ln1_b": nrm(ks[13], (L, D), 0.02),
        "w_router": nrm(ks[14], (L, D, E), D ** -0.5),
        "b_router": nrm(ks[15], (L, E), 0.01),
        "w_gate": nrm(ks[16], (L, E, D, F), D ** -0.5),
        "b_gate": nrm(ks[17], (L, E, F), 0.02),
        "w_up": nrm(ks[18], (L, E, D, F), D ** -0.5),
        "b_up": nrm(ks[19], (L, E, F), 0.02),
        "w_down": nrm(ks[20], (L, E, F, D), DEEPNORM_BETA * F ** -0.5),
        "b_down": nrm(ks[21], (L, E, D), 0.02),
        "ln2_g": 1.0 + nrm(ks[22], (L, D), 0.05),
        "ln2_b": nrm(ks[23], (L, D), 0.02),
    }


def reference(x, c, w_ada, b_ada, w_in, w_gk, b_gk, w_pool, b_pool, pool_scale, gla_norm_w, w_out,
              ln1_g, ln1_b, w_router, b_router, w_gate, b_gate, w_up, b_up, w_down, b_down,
              ln2_g, ln2_b):
    for l in range(DEPTH):
        mod = jax.nn.silu(c) @ w_ada[l] + b_ada[l]
        sh1, sc1, g1, sh2, sc2, g2 = jnp.split(mod[:, None, :], 6, axis=-1)
        h = _modulate(x, sh1, sc1)
        y = _mixer(h, w_in[l], w_gk[l], b_gk[l], w_pool[l], b_pool[l], pool_scale[l],
                   gla_norm_w[l], w_out[l])
        x = _post_ln(DEEPNORM_ALPHA * x + g1 * y, ln1_g[l], ln1_b[l], x.dtype)
        h = _modulate(x, sh2, sc2)
        y = _moe(h, w_router[l], b_router[l], w_gate[l], b_gate[l], w_up[l], b_up[l],
                 w_down[l], b_down[l])
        x = _post_ln(DEEPNORM_ALPHA * x + g2 * y, ln2_g[l], ln2_b[l], x.dtype)
    return x
```

```python
import functools

import jax
import jax.numpy as jnp
from jax import lax
from jax.experimental import pallas as pl
from jax.experimental.pallas import tpu as pltpu

F32 = jnp.float32
BF16 = jnp.bfloat16
I32 = jnp.int32
U32 = jnp.uint32

D_MODEL = 2048
POOL_WIDTH = 1024
POOL_GROUPS = 4
POOL_GROUP_DIM = 256
POOL_WINDOWS = (2, 4, 8, 16)
POOL_HALO = 16
GLA_HEADS = 4
GLA_HEAD_K = 128
GLA_HEAD_V = 256
GLA_K_WIDTH = GLA_HEADS * GLA_HEAD_K
GLA_V_WIDTH = GLA_HEADS * GLA_HEAD_V
GLA_GATE_RANK = 16
GLA_GATE_TAU = 16.0
GLA_CHUNK = 64
GLA_NORM_EPS = 1e-6
N_EXPERTS = 32
TOP_K = 4
D_FF = 2048
SWIGLU_ALPHA = 1.702
SWIGLU_LIMIT = 7.0
DEPTH = 1
DEEPNORM_ALPHA = (2.0 * DEPTH) ** 0.25
LN_EPS = 1e-5

V7X_LANES = 128
V7X_SUBLANES = 8
V7X_VMEM_BYTES = 64 * 1024 * 1024

MOD_TN = 512
MOD_KC = 64
INPROJ_TM = 512
INPROJ_TN = 512
PROJ_WIDTH = 4096 + V7X_LANES
GLA_BLOCK = 256
OUT_TM = 512
DISPATCH_TM = 256
MOE_TM = 256
MOE_SUB = 9
MOE_TF = 256
COMBINE_TM = 128


def _vmem_limit(nbytes):
    return int(min(nbytes, V7X_VMEM_BYTES - 4 * 1024 * 1024))


def _layer_norm(x):
    mu = jnp.mean(x, axis=-1, keepdims=True)
    xc = x - mu
    var = jnp.mean(xc * xc, axis=-1, keepdims=True)
    return xc * lax.rsqrt(var + LN_EPS)


def _mod_kernel(c_ref, w_ref, b_ref, o_ref):
    nb = c_ref.shape[0]
    d, tn = w_ref.shape
    rep = tn // V7X_LANES
    g = MOD_KC // V7X_SUBLANES

    def body(i, accs):
        k0 = pl.multiple_of(i * MOD_KC, MOD_KC)
        w = w_ref[pl.ds(k0, MOD_KC), :].reshape(g, V7X_SUBLANES, tn)
        out = []
        for b in range(nb):
            cb = c_ref[b, pl.ds(k0, MOD_KC), :]
            s = cb * jax.nn.sigmoid(cb)
            s = jnp.concatenate([s] * rep, axis=1).reshape(g, V7X_SUBLANES, tn)
            out.append(accs[b] + jnp.sum(w * s, axis=0))
        return tuple(out)

    init = tuple(jnp.zeros((V7X_SUBLANES, tn), F32) for _ in range(nb))
    accs = lax.fori_loop(0, d // MOD_KC, body, init)
    rows = [jnp.sum(a, axis=0, keepdims=True) for a in accs]
    rows.append(jnp.zeros((V7X_SUBLANES - nb, tn), F32))
    o_ref[...] = jnp.concatenate(rows, axis=0) + b_ref[...]


def _mod_call(c, w_ada, b_ada):
    nb, d = c.shape
    n = w_ada.shape[1]
    c_rep = jnp.broadcast_to(c[:, :, None], (nb, d, V7X_LANES))
    out = pl.pallas_call(
        _mod_kernel,
        grid=(n // MOD_TN,),
        in_specs=[
            pl.BlockSpec((nb, d, V7X_LANES), lambda j: (0, 0, 0)),
            pl.BlockSpec((d, MOD_TN), lambda j: (0, j)),
            pl.BlockSpec((1, MOD_TN), lambda j: (0, j)),
        ],
        out_specs=pl.BlockSpec((V7X_SUBLANES, MOD_TN), lambda j: (0, j)),
        out_shape=jax.ShapeDtypeStruct((V7X_SUBLANES, n), F32),
        compiler_params=pltpu.CompilerParams(dimension_semantics=("arbitrary",)),
        name="mod",
    )(c_rep, w_ada, b_ada.reshape(1, n))
    return out[:nb]


def _inproj_kernel(x_ref, sh_ref, sc_ref, w_ref, o_ref):
    h = _layer_norm(x_ref[...]) * (1.0 + sc_ref[0]) + sh_ref[0]
    hb = h.astype(BF16)
    n = w_ref.shape[1]
    for n0 in range(0, n, INPROJ_TN):
        n1 = min(n0 + INPROJ_TN, n)
        o_ref[:, n0:n1] = jnp.dot(hb, w_ref[:, n0:n1], preferred_element_type=F32)


def _inproj_call(x2, sh1, sc1, w_re, seq):
    t, d = x2.shape
    n = w_re.shape[1]
    tpb = seq // INPROJ_TM
    vmem = 2 * INPROJ_TM * d * 4 + 2 * INPROJ_TM * n * 4 + d * n * 2 + 8 * 1024 * 1024
    return pl.pallas_call(
        _inproj_kernel,
        grid=(t // INPROJ_TM,),
        in_specs=[
            pl.BlockSpec((INPROJ_TM, d), lambda i: (i, 0)),
            pl.BlockSpec((1, 1, d), lambda i: (i // tpb, 0, 0)),
            pl.BlockSpec((1, 1, d), lambda i: (i // tpb, 0, 0)),
            pl.BlockSpec((d, n), lambda i: (0, 0), pipeline_mode=pl.Buffered(1)),
        ],
        out_specs=pl.BlockSpec((INPROJ_TM, n), lambda i: (i, 0)),
        out_shape=jax.ShapeDtypeStruct((t, n), F32),
        compiler_params=pltpu.CompilerParams(
            dimension_semantics=("arbitrary",), vmem_limit_bytes=_vmem_limit(vmem)),
        name="inproj",
    )(x2, sh1, sc1, w_re)


def _chunk_cumsum(x, row_in_chunk):
    s = 1
    while s < GLA_CHUNK:
        x = x + jnp.where(row_in_chunk >= s, pltpu.roll(x, s, 0), 0.0)
        s *= 2
    return x


def _group_ref(b, m, row):
    n, dk = b.shape
    g = 2 * m
    if g >= V7X_SUBLANES:
        r = b.reshape(n // g, g, dk)[:, m - 1:m, :]
        return jnp.broadcast_to(r, (n // g, g, dk)).reshape(n, dk)
    rr = row & (g - 1)
    out = None
    for r in range(g):
        sh = (r - (m - 1)) % n
        shifted = b if sh == 0 else pltpu.roll(b, sh, 0)
        out = shifted if out is None else jnp.where(rr == r, shifted, out)
    return out


def _dot_nt(a, b):
    return lax.dot_general(a, b, (((1,), (1,)), ((), ())), preferred_element_type=F32)


def _gla_kernel(q_ref, k_ref, v_ref, g_ref, lr_ref, wgk_ref, bgk_ref, nw_ref, o_ref, state_ref):
    n = q_ref.shape[0]
    dk, dv = GLA_HEAD_K, GLA_HEAD_V
    n_chunks = n // GLA_CHUNK

    @pl.when(pl.program_id(1) == 0)
    def _():
        state_ref[...] = jnp.zeros_like(state_ref)

    row = lax.broadcasted_iota(I32, (n, dk), 0)
    row_in_chunk = row & (GLA_CHUNK - 1)
    ti = lax.broadcasted_iota(I32, (n, n), 0)
    si = lax.broadcasted_iota(I32, (n, n), 1)
    x = ti ^ si
    level = jnp.zeros((n, n), I32)
    m = 1
    while m < GLA_CHUNK:
        level = level + (x >= m).astype(I32)
        m *= 2
    level = jnp.where((si <= ti) & (x < GLA_CHUNK), level, -1)

    z = jnp.dot(lr_ref[...].astype(BF16), wgk_ref[...], preferred_element_type=F32) + bgk_ref[...]
    log_alpha = (jnp.minimum(z, 0.0) - jnp.log(1.0 + jnp.exp(-jnp.abs(z)))) * (1.0 / GLA_GATE_TAU)

    scale = GLA_HEAD_K ** -0.5
    for h in range(GLA_HEADS):
        q = q_ref[:, h * dk:(h + 1) * dk] * scale
        k = k_ref[:, h * dk:(h + 1) * dk]
        vb = v_ref[:, h * dv:(h + 1) * dv].astype(BF16)
        bc = _chunk_cumsum(log_alpha[:, h * dk:(h + 1) * dk], row_in_chunk)

        scores = jnp.where(level == 0, _dot_nt(q.astype(BF16), k.astype(BF16)), 0.0)
        m = 1
        lvl = 1
        while m < GLA_CHUNK:
            e = jnp.exp(-jnp.abs(bc - _group_ref(bc, m, row)))
            s = _dot_nt((q * e).astype(BF16), (k * e).astype(BF16))
            scores = jnp.where(level == lvl, s, scores)
            m *= 2
            lvl += 1
        o = jnp.dot(scores.astype(BF16), vb, preferred_element_type=F32)

        qd = (q * jnp.exp(bc)).astype(BF16)
        inter = []
        for c in range(n_chunks):
            r0 = c * GLA_CHUNK
            r1 = r0 + GLA_CHUNK
            state = state_ref[h]
            inter.append(jnp.dot(qd[r0:r1], state.astype(BF16), preferred_element_type=F32))
            bcc = bc[r0:r1]
            blast = bcc[GLA_CHUNK - 1:GLA_CHUNK, :]
            kd_t = (k[r0:r1] * jnp.exp(blast - bcc)).T.astype(BF16)
            decay = jnp.exp(bcc.T[:, GLA_CHUNK - 1:GLA_CHUNK])
            state_ref[h] = decay * state + jnp.dot(kd_t, vb[r0:r1], preferred_element_type=F32)
        o = o + jnp.concatenate(inter, axis=0)

        o = o * lax.rsqrt(jnp.mean(o * o, axis=-1, keepdims=True) + GLA_NORM_EPS) * nw_ref[...]
        g = g_ref[:, h * dv:(h + 1) * dv]
        o_ref[:, h * dv:(h + 1) * dv] = (o * (g * jax.nn.sigmoid(g))).astype(o_ref.dtype)


def _gla_call(proj, wgk_pad, bgk, norm_w, batch, seq):
    t = proj.shape[0]
    nb = seq // GLA_BLOCK
    kw, vw = GLA_K_WIDTH, GLA_V_WIDTH
    q_col = POOL_WIDTH // kw
    v_col = (POOL_WIDTH + 2 * kw) // vw
    g_col = v_col + 1
    lr_col = (POOL_WIDTH + 2 * kw + 2 * vw) // V7X_LANES
    row = lambda b, j: b * nb + j
    return pl.pallas_call(
        _gla_kernel,
        grid=(batch, nb),
        in_specs=[
            pl.BlockSpec((GLA_BLOCK, kw), lambda b, j: (row(b, j), q_col)),
            pl.BlockSpec((GLA_BLOCK, kw), lambda b, j: (row(b, j), q_col + 1)),
            pl.BlockSpec((GLA_BLOCK, vw), lambda b, j: (row(b, j), v_col)),
            pl.BlockSpec((GLA_BLOCK, vw), lambda b, j: (row(b, j), g_col)),
            pl.BlockSpec((GLA_BLOCK, V7X_LANES), lambda b, j: (row(b, j), lr_col)),
            pl.BlockSpec((V7X_LANES, kw), lambda b, j: (0, 0)),
            pl.BlockSpec((1, kw), lambda b, j: (0, 0)),
            pl.BlockSpec((1, GLA_HEAD_V), lambda b, j: (0, 0)),
        ],
        out_specs=pl.BlockSpec((GLA_BLOCK, vw), lambda b, j: (row(b, j), 0)),
        out_shape=jax.ShapeDtypeStruct((t, vw), BF16),
        scratch_shapes=[pltpu.VMEM((GLA_HEADS, GLA_HEAD_K, GLA_HEAD_V), F32)],
        compiler_params=pltpu.CompilerParams(dimension_semantics=("arbitrary", "arbitrary")),
        name="gla",
    )(proj, proj, proj, proj, proj, wgk_pad, bgk, norm_w)


def _outproj_kernel(u_ref, yg_ref, x_ref, g1_ref, sh2_ref, sc2_ref, wp_ref, bp_ref, ps_ref, wo_ref,
                    l1g_ref, l1b_ref, wra_ref, wrb_ref, br_ref,
                    x1_ref, h2p_ref, idx_ref, tw_ref, rank_ref, cnt_ref,
                    halo_ref, carry_ref, *, tiles_per_seq):
    i = pl.program_id(0)
    tm = u_ref.shape[0]
    seq_tile = i % tiles_per_seq

    @pl.when(seq_tile == 0)
    def _():
        halo_ref[...] = jnp.zeros_like(halo_ref)

    @pl.when(i == 0)
    def _():
        carry_ref[...] = jnp.zeros_like(carry_ref)

    u = u_ref[...]
    pos1 = seq_tile * tm + lax.broadcasted_iota(I32, (tm, POOL_GROUP_DIM), 0) + 1
    gd = POOL_GROUP_DIM
    y = None
    for gi, win in enumerate(POOL_WINDOWS):
        ug = u[:, gi * gd:(gi + 1) * gd]
        ws = jnp.concatenate([halo_ref[:, gi * gd:(gi + 1) * gd], ug], axis=0)
        s = 1
        while s < win:
            ws = ws + pltpu.roll(ws, s, 0)
            s *= 2
        count = jnp.minimum(pos1, win).astype(F32)
        pooled = ws[POOL_HALO:, :] / count - ug
        mixed = jnp.dot(pooled.astype(BF16), wp_ref[gi], preferred_element_type=F32)
        mixed = (mixed + bp_ref[:, gi * gd:(gi + 1) * gd]) * ps_ref[:, gi * gd:(gi + 1) * gd]
        part = jnp.dot(mixed.astype(BF16), wo_ref[gi * gd:(gi + 1) * gd, :], preferred_element_type=F32)
        y = part if y is None else y + part
    halo_ref[...] = u[tm - POOL_HALO:, :]
    y = y + jnp.dot(yg_ref[...], wo_ref[POOL_WIDTH:, :], preferred_element_type=F32)

    x1 = _layer_norm(DEEPNORM_ALPHA * x_ref[...] + g1_ref[0] * y) * l1g_ref[...] + l1b_ref[...]
    x1_ref[...] = x1
    h2 = _layer_norm(x1) * (1.0 + sc2_ref[0]) + sh2_ref[0]

    hb = h2.astype(BF16)
    hf = hb.astype(F32)
    bits = pltpu.bitcast(hf, U32)
    half = D_MODEL // 2
    h2p_ref[...] = (bits[:, half:] & jnp.uint32(0xFFFF0000)) | (bits[:, :half] >> 16)

    h_lo = (h2 - hf).astype(BF16)
    ra = jnp.dot(hb, wra_ref[...], preferred_element_type=F32)
    rb = jnp.dot(h_lo, wrb_ref[...], preferred_element_type=F32)
    logits = ra + pltpu.roll(ra, V7X_LANES - N_EXPERTS, 1) + rb + br_ref[...]
    lane = lax.broadcasted_iota(I32, (tm, V7X_LANES), 1)
    neg_inf = jnp.float32(-jnp.inf)
    work = jnp.where(lane < N_EXPERTS, logits, neg_inf)
    vals, idxs = [], []
    for _ in range(TOP_K):
        mx = jnp.max(work, axis=-1, keepdims=True)
        sel = jnp.min(jnp.where(work == mx, lane, V7X_LANES), axis=-1, keepdims=True)
        vals.append(mx)
        idxs.append(sel)
        work = jnp.where(lane == sel, neg_inf, work)
    exps = [jnp.exp(v - vals[0]) for v in vals]
    denom = exps[0] + exps[1] + exps[2] + exps[3]

    chosen = jnp.where((lane < N_EXPERTS) & (work == neg_inf), 1.0, 0.0)
    tri = (lax.broadcasted_iota(I32, (tm, tm), 1) < lax.broadcasted_iota(I32, (tm, tm), 0)).astype(BF16)
    excl = jnp.dot(tri, chosen.astype(BF16), preferred_element_type=F32) + carry_ref[0:1, :]
    new_carry = carry_ref[0:1, :] + jnp.sum(chosen, axis=0, keepdims=True)
    carry_ref[...] = jnp.broadcast_to(new_carry, carry_ref.shape)
    cnt_ref[...] = jnp.broadcast_to(new_carry, cnt_ref.shape)

    idx_out = jnp.zeros((tm, V7X_LANES), I32)
    tw_out = jnp.zeros((tm, V7X_LANES), F32)
    rank_out = jnp.zeros((tm, V7X_LANES), F32)
    for kk in range(TOP_K):
        rk = jnp.sum(jnp.where(lane == idxs[kk], excl, 0.0), axis=-1, keepdims=True)
        idx_out = jnp.where(lane == kk, idxs[kk], idx_out)
        tw_out = jnp.where(lane == kk, exps[kk] / denom, tw_out)
        rank_out = jnp.where(lane == kk, rk, rank_out)
    idx_ref[...] = idx_out
    tw_ref[...] = tw_out
    rank_ref[...] = rank_out.astype(I32)


def _outproj_call(proj, y_gla, x2, g1, sh2, sc2, wp, bp, ps, wo, l1g, l1b, wra, wrb, br, seq):
    t, d = x2.shape
    tm = OUT_TM
    tps = seq // tm
    bidx = lambda i: (i // tps, 0, 0)
    const2 = lambda i: (0, 0)
    row = lambda i: (i, 0)
    lanes = V7X_LANES
    vmem = (2 * tm * (POOL_WIDTH * 4 + GLA_V_WIDTH * 2 + d * 4 + d * 4 + d * 2 + 3 * lanes * 4)
            + d * d * 2 + 16 * 1024 * 1024)
    return pl.pallas_call(
        functools.partial(_outproj_kernel, tiles_per_seq=tps),
        grid=(t // tm,),
        in_specs=[
            pl.BlockSpec((tm, POOL_WIDTH), row),
            pl.BlockSpec((tm, GLA_V_WIDTH), row),
            pl.BlockSpec((tm, d), row),
            pl.BlockSpec((1, 1, d), bidx),
            pl.BlockSpec((1, 1, d), bidx),
            pl.BlockSpec((1, 1, d), bidx),
            pl.BlockSpec((POOL_GROUPS, POOL_GROUP_DIM, POOL_GROUP_DIM), lambda i: (0, 0, 0)),
            pl.BlockSpec((1, POOL_WIDTH), const2),
            pl.BlockSpec((1, POOL_WIDTH), const2),
            pl.BlockSpec((d, d), const2, pipeline_mode=pl.Buffered(1)),
            pl.BlockSpec((1, d), const2),
            pl.BlockSpec((1, d), const2),
            pl.BlockSpec((d, lanes), const2),
            pl.BlockSpec((d, lanes), const2),
            pl.BlockSpec((1, lanes), const2),
        ],
        out_specs=[
            pl.BlockSpec((tm, d), row),
            pl.BlockSpec((tm, d // 2), row),
            pl.BlockSpec((tm, lanes), row),
            pl.BlockSpec((tm, lanes), row),
            pl.BlockSpec((tm, lanes), row),
            pl.BlockSpec((V7X_SUBLANES, lanes), const2),
        ],
        out_shape=[
            jax.ShapeDtypeStruct((t, d), F32),
            jax.ShapeDtypeStruct((t, d // 2), U32),
            jax.ShapeDtypeStruct((t, lanes), I32),
            jax.ShapeDtypeStruct((t, lanes), F32),
            jax.ShapeDtypeStruct((t, lanes), I32),
            jax.ShapeDtypeStruct((V7X_SUBLANES, lanes), F32),
        ],
        scratch_shapes=[
            pltpu.VMEM((POOL_HALO, POOL_WIDTH), F32),
            pltpu.VMEM((V7X_SUBLANES, lanes), F32),
        ],
        compiler_params=pltpu.CompilerParams(
            dimension_semantics=("arbitrary",), vmem_limit_bytes=_vmem_limit(vmem)),
        name="outproj",
    )(proj, y_gla, x2, g1, sh2, sc2, wp, bp, ps, wo, l1g, l1b, wra, wrb, br)


def _dispatch_kernel(pos_ref, h_ref, xs_in_ref, xs_ref, sem):
    del xs_in_ref
    tm = h_ref.shape[0]

    def row_copy(t, p):
        return pltpu.make_async_copy(h_ref.at[pl.ds(t, 1)], xs_ref.at[pl.ds(p, 1)], sem)

    def issue(t, carry):
        for kk in range(TOP_K):
            row_copy(t, pos_ref[0, 0, t * TOP_K + kk]).start()
        return carry

    def drain(t, carry):
        for kk in range(TOP_K):
            row_copy(t, pos_ref[0, 0, t * TOP_K + kk]).wait()
        return carry

    lax.fori_loop(0, tm, issue, 0)
    lax.fori_loop(0, tm, drain, 0)


def _dispatch_call(pos, h2p, n_slots):
    t, w = h2p.shape
    tm = DISPATCH_TM
    pos3 = pos.reshape(t // tm, 1, tm * TOP_K)
    xs0 = jnp.zeros((n_slots, w), U32)
    return pl.pallas_call(
        _dispatch_kernel,
        grid=(t // tm,),
        in_specs=[
            pl.BlockSpec((1, 1, tm * TOP_K), lambda i: (i, 0, 0), memory_space=pltpu.SMEM),
            pl.BlockSpec((tm, w), lambda i: (i, 0)),
            pl.BlockSpec(memory_space=pl.ANY),
        ],
        out_specs=pl.BlockSpec(memory_space=pl.ANY),
        out_shape=jax.ShapeDtypeStruct((n_slots, w), U32),
        scratch_shapes=[pltpu.SemaphoreType.DMA(())],
        input_output_aliases={2: 0},
        compiler_params=pltpu.CompilerParams(dimension_semantics=("arbitrary",)),
        name="dispatch",
    )(pos3, h2p, xs0)


def _moe_kernel(item_e_ref, item_row_ref, item_n_ref, tail_ref,
                xs_ref, wg_ref, wu_ref, wd_ref, bg_ref, bu_ref, bd_ref,
                ys_ref,
                xbuf, yacc, wgu_bf, wd_bf, sem_in, sem_out):
    del item_e_ref
    i = pl.program_id(0)
    j = pl.program_id(1)
    n_items = pl.num_programs(0)
    n_chunks = pl.num_programs(1)
    nsub = item_n_ref[i]
    row0 = item_row_ref[i]
    tm, tf = MOE_TM, MOE_TF
    half = D_MODEL // 2

    def in_copy(r):
        src = xs_ref.at[pl.ds(pl.multiple_of(row0 + r * tm, tm), tm)]
        return pltpu.make_async_copy(src, xbuf.at[pl.ds(pl.multiple_of(r * tm, tm), tm)], sem_in.at[r])

    def out_copy(r):
        dst = ys_ref.at[pl.ds(pl.multiple_of(row0 + r * tm, tm), tm)]
        return pltpu.make_async_copy(yacc.at[pl.ds(pl.multiple_of(r * tm, tm), tm)], dst, sem_out.at[r])

    @pl.when((j == 0) & (nsub > 0))
    def _():
        def start(r, c):
            in_copy(r).start()
            return c
        lax.fori_loop(0, nsub, start, 0)

    def chunk(first, last):
        wgu_bf[:, :tf] = wg_ref[0].astype(BF16)
        wgu_bf[:, tf:] = wu_ref[0].astype(BF16)
        wd_bf[...] = wd_ref[0].astype(BF16)

        def sub(r, c):
            if first:
                in_copy(r).wait()
            rows = pl.ds(pl.multiple_of(r * tm, tm), tm)
            xp = xbuf[rows, :]
            x_lo = pltpu.bitcast(xp << 16, F32).astype(BF16)
            x_hi = pltpu.bitcast(xp & jnp.uint32(0xFFFF0000), F32).astype(BF16)
            gu = (jnp.dot(x_lo, wgu_bf[:half, :], preferred_element_type=F32)
                  + jnp.dot(x_hi, wgu_bf[half:, :], preferred_element_type=F32))
            gate = jnp.minimum(gu[:, :tf] + bg_ref[0], SWIGLU_LIMIT)
            up = jnp.clip(gu[:, tf:] + bu_ref[0], -SWIGLU_LIMIT, SWIGLU_LIMIT)
            act = gate * jax.nn.sigmoid(SWIGLU_ALPHA * gate) * (up + 1.0)
            part = jnp.dot(act.astype(BF16), wd_bf[...], preferred_element_type=F32)
            if first:
                yacc[rows, :] = part + bd_ref[0]
            else:
                yacc[rows, :] = yacc[rows, :] + part
            if last:
                out_copy(r).start()
            return c

        lax.fori_loop(0, nsub, sub, 0)
        if last:
            def drain(r, c):
                out_copy(r).wait()
                return c
            lax.fori_loop(0, nsub, drain, 0)

    @pl.when((nsub > 0) & (j == 0))
    def _():
        chunk(True, False)

    @pl.when((nsub > 0) & (j > 0) & (j < n_chunks - 1))
    def _():
        chunk(False, False)

    @pl.when((nsub > 0) & (j == n_chunks - 1))
    def _():
        chunk(False, True)

    @pl.when((i == n_items - 1) & (j == n_chunks - 1))
    def _():
        yacc[0:tm, :] = jnp.zeros((tm, D_MODEL), F32)
        tail0 = tail_ref[0]
        n_tail = ys_ref.shape[0] // tm - tail0

        def tail_copy(r):
            dst = ys_ref.at[pl.ds(pl.multiple_of((tail0 + r) * tm, tm), tm)]
            return pltpu.make_async_copy(yacc.at[0:tm], dst, sem_out.at[0])

        def start(r, c):
            tail_copy(r).start()
            return c

        def drain(r, c):
            tail_copy(r).wait()
            return c

        lax.fori_loop(0, n_tail, start, 0)
        lax.fori_loop(0, n_tail, drain, 0)


def _moe_call(item_e, item_row, item_n, tail_tile, xs, wg, wu, wd, bg, bu, bd):
    n_slots, half = xs.shape
    d = 2 * half
    e, _, f = wg.shape
    n_items = item_e.shape[0]
    n_chunks = f // MOE_TF
    rows = MOE_SUB * MOE_TM

    def jeff(i, j, n_ref):
        return jnp.where(n_ref[i] > 0, j, n_chunks - 1)

    vmem = (rows * half * 4 + rows * d * 4 + 3 * 2 * d * MOE_TF * 4 + 3 * d * MOE_TF * 2
            + 6 * 1024 * 1024)
    grid_spec = pltpu.PrefetchScalarGridSpec(
        num_scalar_prefetch=4,
        grid=(n_items, n_chunks),
        in_specs=[
            pl.BlockSpec(memory_space=pl.ANY),
            pl.BlockSpec((1, d, MOE_TF), lambda i, j, e_r, r_r, n_r, t_r: (e_r[i], 0, jeff(i, j, n_r))),
            pl.BlockSpec((1, d, MOE_TF), lambda i, j, e_r, r_r, n_r, t_r: (e_r[i], 0, jeff(i, j, n_r))),
            pl.BlockSpec((1, MOE_TF, d), lambda i, j, e_r, r_r, n_r, t_r: (e_r[i], jeff(i, j, n_r), 0)),
            pl.BlockSpec((1, 1, MOE_TF), lambda i, j, e_r, r_r, n_r, t_r: (e_r[i], 0, jeff(i, j, n_r))),
            pl.BlockSpec((1, 1, MOE_TF), lambda i, j, e_r, r_r, n_r, t_r: (e_r[i], 0, jeff(i, j, n_r))),
            pl.BlockSpec((1, 1, d), lambda i, j, e_r, r_r, n_r, t_r: (e_r[i], 0, 0)),
        ],
        out_specs=pl.BlockSpec(memory_space=pl.ANY),
        scratch_shapes=[
            pltpu.VMEM((rows, half), U32),
            pltpu.VMEM((rows, d), F32),
            pltpu.VMEM((d, 2 * MOE_TF), BF16),
            pltpu.VMEM((MOE_TF, d), BF16),
            pltpu.SemaphoreType.DMA((MOE_SUB,)),
            pltpu.SemaphoreType.DMA((MOE_SUB,)),
        ],
    )
    return pl.pallas_call(
        _moe_kernel,
        grid_spec=grid_spec,
        out_shape=jax.ShapeDtypeStruct((n_slots, d), F32),
        compiler_params=pltpu.CompilerParams(
            dimension_semantics=("arbitrary", "arbitrary"), vmem_limit_bytes=_vmem_limit(vmem)),
        name="moe",
    )(item_e, item_row, item_n, tail_tile, xs, wg, wu, wd, bg.reshape(e, 1, f), bu.reshape(e, 1, f),
      bd.reshape(e, 1, d))


def _combine_kernel(pos_ref, posn_ref, ys_ref, x1_ref, tw_ref, g2_ref, l2g_ref, l2b_ref,
                    o_ref, buf, sem):
    i = pl.program_id(0)
    n = pl.num_programs(0)
    tm = x1_ref.shape[0]
    slot = i % 2

    def row_copy(p_ref, t, kk, s):
        p = p_ref[0, 0, t * TOP_K + kk]
        return pltpu.make_async_copy(ys_ref.at[pl.ds(p, 1)], buf.at[s, kk, pl.ds(t, 1)], sem.at[s])

    def issue(p_ref, s):
        def body(t, c):
            for kk in range(TOP_K):
                row_copy(p_ref, t, kk, s).start()
            return c
        lax.fori_loop(0, tm, body, 0)

    @pl.when(i == 0)
    def _():
        issue(pos_ref, 0)

    @pl.when(i + 1 < n)
    def _():
        issue(posn_ref, 1 - slot)

    def drain(t, c):
        for kk in range(TOP_K):
            row_copy(pos_ref, t, kk, slot).wait()
        return c
    lax.fori_loop(0, tm, drain, 0)

    tw = tw_ref[...]
    y = None
    for kk in range(TOP_K):
        part = tw[:, kk:kk + 1] * buf[slot, kk]
        y = part if y is None else y + part
    r = DEEPNORM_ALPHA * x1_ref[...] + g2_ref[0] * y
    o_ref[...] = _layer_norm(r) * l2g_ref[...] + l2b_ref[...]


def _combine_call(pos, ys, x1, tw, g2, l2g, l2b, seq):
    t, d = x1.shape
    tm = COMBINE_TM
    n = t // tm
    tps = seq // tm
    pos3 = pos.reshape(n, 1, tm * TOP_K)
    lanes = V7X_LANES
    vmem = 2 * TOP_K * tm * d * 4 + 4 * tm * d * 4 + 8 * 1024 * 1024
    return pl.pallas_call(
        _combine_kernel,
        grid=(n,),
        in_specs=[
            pl.BlockSpec((1, 1, tm * TOP_K), lambda i: (i, 0, 0), memory_space=pltpu.SMEM),
            pl.BlockSpec((1, 1, tm * TOP_K), lambda i: (jnp.minimum(i + 1, n - 1), 0, 0),
                         memory_space=pltpu.SMEM),
            pl.BlockSpec(memory_space=pl.ANY),
            pl.BlockSpec((tm, d), lambda i: (i, 0)),
            pl.BlockSpec((tm, lanes), lambda i: (i, 0)),
            pl.BlockSpec((1, 1, d), lambda i: (i // tps, 0, 0)),
            pl.BlockSpec((1, d), lambda i: (0, 0)),
            pl.BlockSpec((1, d), lambda i: (0, 0)),
        ],
        out_specs=pl.BlockSpec((tm, d), lambda i: (i, 0)),
        out_shape=jax.ShapeDtypeStruct((t, d), F32),
        scratch_shapes=[
            pltpu.VMEM((2, TOP_K, tm, d), F32),
            pltpu.SemaphoreType.DMA((2,)),
        ],
        compiler_params=pltpu.CompilerParams(
            dimension_semantics=("arbitrary",), vmem_limit_bytes=_vmem_limit(vmem)),
        name="combine",
    )(pos3, pos3, ys, x1, tw, g2, l2g, l2b)


def _routing_tables(counts, idx4, rank4, n_items):
    e = counts.shape[0]
    padded = (counts + MOE_TM - 1) // MOE_TM * MOE_TM
    pad_end = jnp.cumsum(padded)
    pad_start = pad_end - padded
    pos = pad_start[idx4] + rank4

    nsub_e = padded // MOE_TM
    items_e = (nsub_e + MOE_SUB - 1) // MOE_SUB
    item_end = jnp.cumsum(items_e)
    item_start = item_end - items_e
    total = item_end[-1]
    w = jnp.arange(n_items, dtype=I32)
    last = jnp.maximum(total - 1, 0)
    wv = jnp.minimum(w, last)
    ew = jnp.minimum(jnp.searchsorted(item_end, wv, side="right"), e - 1).astype(I32)
    s = wv - item_start[ew]
    valid = w < total
    item_row = jnp.where(valid, pad_start[ew] + s * (MOE_SUB * MOE_TM), 0).astype(I32)
    item_n = jnp.where(valid, jnp.clip(nsub_e[ew] - s * MOE_SUB, 0, MOE_SUB), 0).astype(I32)
    tail_tile = (pad_end[-1:] // MOE_TM).astype(I32)
    return pos.astype(I32), ew, item_row, item_n, tail_tile


def kernel(x, c, w_ada, b_ada, w_in, w_gk, b_gk, w_pool, b_pool, pool_scale, gla_norm_w, w_out,
           ln1_g, ln1_b, w_router, b_router, w_gate, b_gate, w_up, b_up, w_down, b_down,
           ln2_g, ln2_b):
    batch, seq, d = x.shape
    t = batch * seq
    e = w_router.shape[-1]
    lanes = V7X_LANES
    for l in range(w_ada.shape[0]):
        mod = _mod_call(c, w_ada[l], b_ada[l])
        sh1, sc1, g1, sh2, sc2, g2 = [m.reshape(batch, 1, d) for m in jnp.split(mod, 6, axis=-1)]

        wi = w_in[l]
        lr0 = POOL_WIDTH + 2 * GLA_K_WIDTH + GLA_V_WIDTH
        w_re = jnp.concatenate(
            [wi[:, :lr0], wi[:, lr0 + GLA_GATE_RANK:], wi[:, lr0:lr0 + GLA_GATE_RANK],
             jnp.zeros((d, lanes - GLA_GATE_RANK), wi.dtype)], axis=1).astype(BF16)
        x2 = x.reshape(t, d)
        proj = _inproj_call(x2, sh1, sc1, w_re, seq)

        wgk_pad = jnp.concatenate(
            [w_gk[l], jnp.zeros((lanes - GLA_GATE_RANK, GLA_K_WIDTH), w_gk.dtype)], axis=0).astype(BF16)
        y_gla = _gla_call(proj, wgk_pad, b_gk[l].reshape(1, -1), gla_norm_w[l].reshape(1, -1), batch, seq)

        wr = w_router[l]
        wr_hi = wr.astype(BF16)
        wr_lo = (wr - wr_hi.astype(F32)).astype(BF16)
        wra = jnp.concatenate([wr_hi, wr_lo, jnp.zeros((d, lanes - 2 * e), BF16)], axis=1)
        wrb = jnp.concatenate([wr_hi, jnp.zeros((d, lanes - e), BF16)], axis=1)
        br = jnp.concatenate([b_router[l], jnp.zeros((lanes - e,), F32)]).reshape(1, lanes)
        x1, h2p, idx, tw, rank, cnt = _outproj_call(
            proj, y_gla, x2, g1, sh2, sc2,
            w_pool[l].astype(BF16), b_pool[l].reshape(1, -1), pool_scale[l].reshape(1, -1),
            w_out[l].astype(BF16), ln1_g[l].reshape(1, -1), ln1_b[l].reshape(1, -1),
            wra, wrb, br, seq)

        n_slots = t * TOP_K + e * MOE_TM
        n_items = e + n_slots // (MOE_SUB * MOE_TM)
        counts = cnt[0, :e].astype(I32)
        pos, item_e, item_row, item_n, tail_tile = _routing_tables(
            counts, idx[:, :TOP_K], rank[:, :TOP_K], n_items)

        xs = _dispatch_call(pos, h2p, n_slots)
        ys = _moe_call(item_e, item_row, item_n, tail_tile, xs, w_gate[l], w_up[l], w_down[l],
                       b_gate[l], b_up[l], b_down[l])
        out = _combine_call(pos, ys, x1, tw, g2, ln2_g[l].reshape(1, -1), ln2_b[l].reshape(1, -1), seq)
        x = out.reshape(batch, seq, d)
    return x
```

```python
import functools

import jax
import jax.numpy as jnp
from jax import lax
from jax.experimental import pallas as pl
from jax.experimental.pallas import tpu as pltpu

F32 = jnp.float32
BF16 = jnp.bfloat16
I32 = jnp.int32

D_MODEL = 2048
POOL_WIDTH = 1024
POOL_GROUPS = 4
POOL_GROUP_DIM = 256
POOL_WINDOWS = (2, 4, 8, 16)
POOL_HALO = 16
GLA_HEADS = 4
GLA_HEAD_K = 128
GLA_HEAD_V = 256
GLA_K_WIDTH = GLA_HEADS * GLA_HEAD_K
GLA_V_WIDTH = GLA_HEADS * GLA_HEAD_V
GLA_GATE_RANK = 16
GLA_GATE_TAU = 16.0
GLA_CHUNK = 64
GLA_NORM_EPS = 1e-6
N_EXPERTS = 32
TOP_K = 4
D_FF = 2048
SWIGLU_ALPHA = 1.702
SWIGLU_LIMIT = 7.0
DEPTH = 1
DEEPNORM_ALPHA = (2.0 * DEPTH) ** 0.25
LN_EPS = 1e-5

V7X_LANES = 128
V7X_SUBLANES = 8
V7X_VMEM_BYTES = 64 * 1024 * 1024

MOD_TN = 512
MOD_KC = 64
INPROJ_TM = 512
INPROJ_TN = 512
PROJ_WIDTH = 4096 + V7X_LANES
GLA_BLOCK = 256
OUT_TM = 512
OUT_PARTS = 1
DISPATCH_TM = 256
MOE_TM = 256
MOE_SUB = 9
MOE_TF = 256
COMBINE_TM = 128


def _vmem_limit(nbytes):
    return int(min(nbytes, V7X_VMEM_BYTES - 4 * 1024 * 1024))


def _layer_norm(x):
    mu = jnp.mean(x, axis=-1, keepdims=True)
    xc = x - mu
    var = jnp.mean(xc * xc, axis=-1, keepdims=True)
    return xc * lax.rsqrt(var + LN_EPS)


def _mod_kernel(c_ref, w_ref, b_ref, o_ref):
    nb = c_ref.shape[0]
    d, tn = w_ref.shape
    rep = tn // V7X_LANES
    g = MOD_KC // V7X_SUBLANES

    def body(i, accs):
        k0 = pl.multiple_of(i * MOD_KC, MOD_KC)
        w = w_ref[pl.ds(k0, MOD_KC), :].reshape(g, V7X_SUBLANES, tn)
        out = []
        for b in range(nb):
            cb = c_ref[b, pl.ds(k0, MOD_KC), :]
            s = cb * jax.nn.sigmoid(cb)
            s = jnp.concatenate([s] * rep, axis=1).reshape(g, V7X_SUBLANES, tn)
            out.append(accs[b] + jnp.sum(w * s, axis=0))
        return tuple(out)

    init = tuple(jnp.zeros((V7X_SUBLANES, tn), F32) for _ in range(nb))
    accs = lax.fori_loop(0, d // MOD_KC, body, init)
    rows = [jnp.sum(a, axis=0, keepdims=True) for a in accs]
    rows.append(jnp.zeros((V7X_SUBLANES - nb, tn), F32))
    o_ref[...] = jnp.concatenate(rows, axis=0) + b_ref[...]


def _mod_call(c, w_ada, b_ada):
    nb, d = c.shape
    n = w_ada.shape[1]
    c_rep = jnp.broadcast_to(c[:, :, None], (nb, d, V7X_LANES))
    out = pl.pallas_call(
        _mod_kernel,
        grid=(n // MOD_TN,),
        in_specs=[
            pl.BlockSpec((nb, d, V7X_LANES), lambda j: (0, 0, 0)),
            pl.BlockSpec((d, MOD_TN), lambda j: (0, j)),
            pl.BlockSpec((1, MOD_TN), lambda j: (0, j)),
        ],
        out_specs=pl.BlockSpec((V7X_SUBLANES, MOD_TN), lambda j: (0, j)),
        out_shape=jax.ShapeDtypeStruct((V7X_SUBLANES, n), F32),
        compiler_params=pltpu.CompilerParams(dimension_semantics=("arbitrary",)),
        name="mod",
    )(c_rep, w_ada, b_ada.reshape(1, n))
    return out[:nb]


def _inproj_kernel(x_ref, sh_ref, sc_ref, w_ref, o_ref):
    h = _layer_norm(x_ref[...]) * (1.0 + sc_ref[0]) + sh_ref[0]
    hb = h.astype(BF16)
    n = w_ref.shape[1]
    for n0 in range(0, n, INPROJ_TN):
        n1 = min(n0 + INPROJ_TN, n)
        o_ref[:, n0:n1] = jnp.dot(hb, w_ref[:, n0:n1], preferred_element_type=F32)


def _inproj_call(x2, sh1, sc1, w_re, seq):
    t, d = x2.shape
    n = w_re.shape[1]
    tpb = seq // INPROJ_TM
    vmem = 2 * INPROJ_TM * d * 4 + 2 * INPROJ_TM * n * 4 + d * n * 2 + 8 * 1024 * 1024
    return pl.pallas_call(
        _inproj_kernel,
        grid=(t // INPROJ_TM,),
        in_specs=[
            pl.BlockSpec((INPROJ_TM, d), lambda i: (i, 0)),
            pl.BlockSpec((1, 1, d), lambda i: (i // tpb, 0, 0)),
            pl.BlockSpec((1, 1, d), lambda i: (i // tpb, 0, 0)),
            pl.BlockSpec((d, n), lambda i: (0, 0), pipeline_mode=pl.Buffered(1)),
        ],
        out_specs=pl.BlockSpec((INPROJ_TM, n), lambda i: (i, 0)),
        out_shape=jax.ShapeDtypeStruct((t, n), F32),
        compiler_params=pltpu.CompilerParams(
            dimension_semantics=("arbitrary",), vmem_limit_bytes=_vmem_limit(vmem)),
        name="inproj",
    )(x2, sh1, sc1, w_re)


def _chunk_cumsum(x, row_in_chunk):
    s = 1
    while s < GLA_CHUNK:
        x = x + jnp.where(row_in_chunk >= s, pltpu.roll(x, s, 0), 0.0)
        s *= 2
    return x


def _group_ref(b, m, row):
    n, dk = b.shape
    g = 2 * m
    if g >= V7X_SUBLANES:
        r = b.reshape(n // g, g, dk)[:, m - 1:m, :]
        return jnp.broadcast_to(r, (n // g, g, dk)).reshape(n, dk)
    rr = row & (g - 1)
    out = None
    for r in range(g):
        sh = (r - (m - 1)) % n
        shifted = b if sh == 0 else pltpu.roll(b, sh, 0)
        out = shifted if out is None else jnp.where(rr == r, shifted, out)
    return out


def _dot_nt(a, b):
    return lax.dot_general(a, b, (((1,), (1,)), ((), ())), preferred_element_type=F32)


def _gla_kernel(q_ref, k_ref, v_ref, g_ref, lr_ref, wgk_ref, bgk_ref, nw_ref, o_ref, state_ref):
    n = q_ref.shape[0]
    dk, dv = GLA_HEAD_K, GLA_HEAD_V
    n_chunks = n // GLA_CHUNK

    @pl.when(pl.program_id(1) == 0)
    def _():
        state_ref[...] = jnp.zeros_like(state_ref)

    row = lax.broadcasted_iota(I32, (n, dk), 0)
    row_in_chunk = row & (GLA_CHUNK - 1)
    ti = lax.broadcasted_iota(I32, (n, n), 0)
    si = lax.broadcasted_iota(I32, (n, n), 1)
    x = ti ^ si
    level = jnp.zeros((n, n), I32)
    m = 1
    while m < GLA_CHUNK:
        level = level + (x >= m).astype(I32)
        m *= 2
    level = jnp.where((si <= ti) & (x < GLA_CHUNK), level, -1)

    z = jnp.dot(lr_ref[...].astype(BF16), wgk_ref[...], preferred_element_type=F32) + bgk_ref[...]
    log_alpha = (jnp.minimum(z, 0.0) - jnp.log(1.0 + jnp.exp(-jnp.abs(z)))) * (1.0 / GLA_GATE_TAU)

    scale = GLA_HEAD_K ** -0.5
    for h in range(GLA_HEADS):
        q = q_ref[:, h * dk:(h + 1) * dk] * scale
        k = k_ref[:, h * dk:(h + 1) * dk]
        vb = v_ref[:, h * dv:(h + 1) * dv].astype(BF16)
        bc = _chunk_cumsum(log_alpha[:, h * dk:(h + 1) * dk], row_in_chunk)

        scores = jnp.where(level == 0, _dot_nt(q.astype(BF16), k.astype(BF16)), 0.0)
        m = 1
        lvl = 1
        while m < GLA_CHUNK:
            e = jnp.exp(-jnp.abs(bc - _group_ref(bc, m, row)))
            s = _dot_nt((q * e).astype(BF16), (k * e).astype(BF16))
            scores = jnp.where(level == lvl, s, scores)
            m *= 2
            lvl += 1
        o = jnp.dot(scores.astype(BF16), vb, preferred_element_type=F32)

        qd = (q * jnp.exp(bc)).astype(BF16)
        inter = []
        for c in range(n_chunks):
            r0 = c * GLA_CHUNK
            r1 = r0 + GLA_CHUNK
            state = state_ref[h]
            inter.append(jnp.dot(qd[r0:r1], state.astype(BF16), preferred_element_type=F32))
            bcc = bc[r0:r1]
            blast = bcc[GLA_CHUNK - 1:GLA_CHUNK, :]
            kd_t = (k[r0:r1] * jnp.exp(blast - bcc)).T.astype(BF16)
            decay = jnp.exp(bcc.T[:, GLA_CHUNK - 1:GLA_CHUNK])
            state_ref[h] = decay * state + jnp.dot(kd_t, vb[r0:r1], preferred_element_type=F32)
        o = o + jnp.concatenate(inter, axis=0)

        o = o * lax.rsqrt(jnp.mean(o * o, axis=-1, keepdims=True) + GLA_NORM_EPS) * nw_ref[...]
        g = g_ref[:, h * dv:(h + 1) * dv]
        o_ref[:, h * dv:(h + 1) * dv] = (o * (g * jax.nn.sigmoid(g))).astype(o_ref.dtype)


def _gla_call(proj, wgk_pad, bgk, norm_w, batch, seq):
    t = proj.shape[0]
    nb = seq // GLA_BLOCK
    kw, vw = GLA_K_WIDTH, GLA_V_WIDTH
    q_col = POOL_WIDTH // kw
    v_col = (POOL_WIDTH + 2 * kw) // vw
    g_col = v_col + 1
    lr_col = (POOL_WIDTH + 2 * kw + 2 * vw) // V7X_LANES
    row = lambda b, j: b * nb + j
    return pl.pallas_call(
        _gla_kernel,
        grid=(batch, nb),
        in_specs=[
            pl.BlockSpec((GLA_BLOCK, kw), lambda b, j: (row(b, j), q_col)),
            pl.BlockSpec((GLA_BLOCK, kw), lambda b, j: (row(b, j), q_col + 1)),
            pl.BlockSpec((GLA_BLOCK, vw), lambda b, j: (row(b, j), v_col)),
            pl.BlockSpec((GLA_BLOCK, vw), lambda b, j: (row(b, j), g_col)),
            pl.BlockSpec((GLA_BLOCK, V7X_LANES), lambda b, j: (row(b, j), lr_col)),
            pl.BlockSpec((V7X_LANES, kw), lambda b, j: (0, 0)),
            pl.BlockSpec((1, kw), lambda b, j: (0, 0)),
            pl.BlockSpec((1, GLA_HEAD_V), lambda b, j: (0, 0)),
        ],
        out_specs=pl.BlockSpec((GLA_BLOCK, vw), lambda b, j: (row(b, j), 0)),
        out_shape=jax.ShapeDtypeStruct((t, vw), BF16),
        scratch_shapes=[pltpu.VMEM((GLA_HEADS, GLA_HEAD_K, GLA_HEAD_V), F32)],
        compiler_params=pltpu.CompilerParams(dimension_semantics=("arbitrary", "arbitrary")),
        name="gla",
    )(proj, proj, proj, proj, proj, wgk_pad, bgk, norm_w)


def _outproj_kernel(u_ref, yg_ref, x_ref, g1_ref, sh2_ref, sc2_ref, wp_ref, bp_ref, ps_ref, wo_ref,
                    l1g_ref, l1b_ref, wra_ref, wrb_ref, br_ref,
                    x1_ref, h2_ref, idx_ref, tw_ref, rank_ref, cnt_ref,
                    halo_ref, carry_ref, *, tiles_per_seq):
    i = pl.program_id(0)
    tm = u_ref.shape[0]
    seq_tile = i % tiles_per_seq

    @pl.when(seq_tile == 0)
    def _():
        halo_ref[...] = jnp.zeros_like(halo_ref)

    @pl.when(i == 0)
    def _():
        carry_ref[...] = jnp.zeros_like(carry_ref)

    gd = POOL_GROUP_DIM
    part_rows = tm // OUT_PARTS
    lane = lax.broadcasted_iota(I32, (part_rows, V7X_LANES), 1)
    neg_inf = jnp.float32(-jnp.inf)
    tri = (lax.broadcasted_iota(I32, (part_rows, part_rows), 1)
           < lax.broadcasted_iota(I32, (part_rows, part_rows), 0)).astype(BF16)

    def part(r0, halo, carry):
        rows = slice(r0, r0 + part_rows)
        u = u_ref[rows, :]
        pos1 = seq_tile * tm + r0 + lax.broadcasted_iota(I32, (part_rows, gd), 0) + 1
        mixed = []
        for gi, win in enumerate(POOL_WINDOWS):
            cols = slice(gi * gd, (gi + 1) * gd)
            ug = u[:, cols]
            ws = jnp.concatenate([halo[:, cols], ug], axis=0)
            s = 1
            while s < win:
                ws = ws + pltpu.roll(ws, s, 0)
                s *= 2
            count = jnp.minimum(pos1, win).astype(F32)
            pooled = ws[POOL_HALO:, :] / count - ug
            m = jnp.dot(pooled.astype(BF16), wp_ref[gi], preferred_element_type=F32)
            mixed.append(((m + bp_ref[:, cols]) * ps_ref[:, cols]).astype(BF16))
        cat = jnp.concatenate(mixed + [yg_ref[rows, :]], axis=1)
        y = jnp.dot(cat, wo_ref[...], preferred_element_type=F32)

        x1 = _layer_norm(DEEPNORM_ALPHA * x_ref[rows, :] + g1_ref[0] * y) * l1g_ref[...] + l1b_ref[...]
        x1_ref[rows, :] = x1
        h2 = _layer_norm(x1) * (1.0 + sc2_ref[0]) + sh2_ref[0]
        h2_ref[rows, :] = h2

        hb = h2.astype(BF16)
        h_lo = (h2 - hb.astype(F32)).astype(BF16)
        ra = jnp.dot(hb, wra_ref[...], preferred_element_type=F32)
        rb = jnp.dot(h_lo, wrb_ref[...], preferred_element_type=F32)
        logits = ra + pltpu.roll(ra, V7X_LANES - N_EXPERTS, 1) + rb + br_ref[...]
        work = jnp.where(lane < N_EXPERTS, logits, neg_inf)
        vals, idxs = [], []
        for _ in range(TOP_K):
            mx = jnp.max(work, axis=-1, keepdims=True)
            sel = jnp.min(jnp.where(work == mx, lane, V7X_LANES), axis=-1, keepdims=True)
            vals.append(mx)
            idxs.append(sel)
            work = jnp.where(lane == sel, neg_inf, work)
        exps = [jnp.exp(v - vals[0]) for v in vals]
        denom = exps[0] + exps[1] + exps[2] + exps[3]

        chosen = jnp.where((lane < N_EXPERTS) & (work == neg_inf), 1.0, 0.0)
        excl = jnp.dot(tri, chosen.astype(BF16), preferred_element_type=F32) + carry
        idx_out = jnp.zeros((part_rows, V7X_LANES), I32)
        tw_out = jnp.zeros((part_rows, V7X_LANES), F32)
        rank_out = jnp.zeros((part_rows, V7X_LANES), F32)
        for kk in range(TOP_K):
            rk = jnp.sum(jnp.where(lane == idxs[kk], excl, 0.0), axis=-1, keepdims=True)
            idx_out = jnp.where(lane == kk, idxs[kk], idx_out)
            tw_out = jnp.where(lane == kk, exps[kk] / denom, tw_out)
            rank_out = jnp.where(lane == kk, rk, rank_out)
        idx_ref[rows, :] = idx_out
        tw_ref[rows, :] = tw_out
        rank_ref[rows, :] = rank_out.astype(I32)
        return carry + jnp.sum(chosen, axis=0, keepdims=True)

    carry = carry_ref[0:1, :]
    halo = halo_ref[...]
    for p in range(OUT_PARTS):
        r0 = p * part_rows
        carry = part(r0, halo, carry)
        halo = u_ref[r0 + part_rows - POOL_HALO:r0 + part_rows, :]
    halo_ref[...] = halo
    carry_ref[...] = jnp.broadcast_to(carry, carry_ref.shape)
    cnt_ref[...] = jnp.broadcast_to(carry, cnt_ref.shape)


def _outproj_call(proj, y_gla, x2, g1, sh2, sc2, wp, bp, ps, wo, l1g, l1b, wra, wrb, br, seq):
    t, d = x2.shape
    tm = OUT_TM
    tps = seq // tm
    bidx = lambda i: (i // tps, 0, 0)
    const2 = lambda i: (0, 0)
    row = lambda i: (i, 0)
    lanes = V7X_LANES
    vmem = (2 * tm * (POOL_WIDTH * 4 + GLA_V_WIDTH * 2 + d * 4 + d * 4 + d * 4 + 3 * lanes * 4)
            + d * d * 2 + 16 * 1024 * 1024)
    return pl.pallas_call(
        functools.partial(_outproj_kernel, tiles_per_seq=tps),
        grid=(t // tm,),
        in_specs=[
            pl.BlockSpec((tm, POOL_WIDTH), row),
            pl.BlockSpec((tm, GLA_V_WIDTH), row),
            pl.BlockSpec((tm, d), row),
            pl.BlockSpec((1, 1, d), bidx),
            pl.BlockSpec((1, 1, d), bidx),
            pl.BlockSpec((1, 1, d), bidx),
            pl.BlockSpec((POOL_GROUPS, POOL_GROUP_DIM, POOL_GROUP_DIM), lambda i: (0, 0, 0)),
            pl.BlockSpec((1, POOL_WIDTH), const2),
            pl.BlockSpec((1, POOL_WIDTH), const2),
            pl.BlockSpec((d, d), const2, pipeline_mode=pl.Buffered(1)),
            pl.BlockSpec((1, d), const2),
            pl.BlockSpec((1, d), const2),
            pl.BlockSpec((d, lanes), const2),
            pl.BlockSpec((d, lanes), const2),
            pl.BlockSpec((1, lanes), const2),
        ],
        out_specs=[
            pl.BlockSpec((tm, d), row),
            pl.BlockSpec((tm, d), row),
            pl.BlockSpec((tm, lanes), row),
            pl.BlockSpec((tm, lanes), row),
            pl.BlockSpec((tm, lanes), row),
            pl.BlockSpec((V7X_SUBLANES, lanes), const2),
        ],
        out_shape=[
            jax.ShapeDtypeStruct((t, d), F32),
            jax.ShapeDtypeStruct((t, d), F32),
            jax.ShapeDtypeStruct((t, lanes), I32),
            jax.ShapeDtypeStruct((t, lanes), F32),
            jax.ShapeDtypeStruct((t, lanes), I32),
            jax.ShapeDtypeStruct((V7X_SUBLANES, lanes), F32),
        ],
        scratch_shapes=[
            pltpu.VMEM((POOL_HALO, POOL_WIDTH), F32),
            pltpu.VMEM((V7X_SUBLANES, lanes), F32),
        ],
        compiler_params=pltpu.CompilerParams(
            dimension_semantics=("arbitrary",), vmem_limit_bytes=_vmem_limit(vmem)),
        name="outproj",
    )(proj, y_gla, x2, g1, sh2, sc2, wp, bp, ps, wo, l1g, l1b, wra, wrb, br)


def _dispatch_kernel(tbl_ref, pos_ref, h_ref, xs_ref, zero_ref, sem, zsem):
    tm = h_ref.shape[0]
    n_experts = tbl_ref.shape[1]
    n_tiles = xs_ref.shape[0] // MOE_TM

    def pad_copy(p):
        return pltpu.make_async_copy(zero_ref.at[pl.ds(0, 1)], xs_ref.at[pl.ds(p, 1)], zsem)

    def tail_copy(tile):
        dst = xs_ref.at[pl.ds(pl.multiple_of(tile * MOE_TM, MOE_TM), MOE_TM)]
        return pltpu.make_async_copy(zero_ref, dst, zsem)

    def for_each_pad(fn):
        def per_expert(e, c):
            cnt = tbl_ref[0, e]
            first = tbl_ref[1, e] + cnt
            n_pad = (cnt + MOE_TM - 1) // MOE_TM * MOE_TM - cnt

            def per_row(r, c2):
                fn(pad_copy(first + r))
                return c2
            lax.fori_loop(0, n_pad, per_row, 0)
            return c
        lax.fori_loop(0, n_experts, per_expert, 0)
        last = n_experts - 1
        tail0 = (tbl_ref[1, last] + tbl_ref[0, last] + MOE_TM - 1) // MOE_TM

        def per_tile(r, c):
            fn(tail_copy(tail0 + r))
            return c
        lax.fori_loop(0, n_tiles - tail0, per_tile, 0)

    @pl.when(pl.program_id(0) == 0)
    def _():
        zero_ref[...] = jnp.zeros_like(zero_ref)
        for_each_pad(lambda cp: cp.start())
        for_each_pad(lambda cp: cp.wait())

    def row_copy(t, p):
        return pltpu.make_async_copy(h_ref.at[pl.ds(t, 1)], xs_ref.at[pl.ds(p, 1)], sem)

    def issue(t, carry):
        for kk in range(TOP_K):
            row_copy(t, pos_ref[0, 0, t * TOP_K + kk]).start()
        return carry

    def drain(t, carry):
        for kk in range(TOP_K):
            row_copy(t, pos_ref[0, 0, t * TOP_K + kk]).wait()
        return carry

    lax.fori_loop(0, tm, issue, 0)
    lax.fori_loop(0, tm, drain, 0)


def _dispatch_call(tbl, pos, h2, n_slots):
    t, w = h2.shape
    tm = DISPATCH_TM
    pos3 = pos.reshape(t // tm, 1, tm * TOP_K)
    return pl.pallas_call(
        _dispatch_kernel,
        grid=(t // tm,),
        in_specs=[
            pl.BlockSpec(memory_space=pltpu.SMEM),
            pl.BlockSpec((1, 1, tm * TOP_K), lambda i: (i, 0, 0), memory_space=pltpu.SMEM),
            pl.BlockSpec((tm, w), lambda i: (i, 0)),
        ],
        out_specs=pl.BlockSpec(memory_space=pl.ANY),
        out_shape=jax.ShapeDtypeStruct((n_slots, w), h2.dtype),
        scratch_shapes=[
            pltpu.VMEM((MOE_TM, w), h2.dtype),
            pltpu.SemaphoreType.DMA(()),
            pltpu.SemaphoreType.DMA(()),
        ],
        compiler_params=pltpu.CompilerParams(dimension_semantics=("arbitrary",)),
        name="dispatch",
    )(tbl, pos3, h2)


def _moe_kernel(item_e_ref, item_row_ref, item_n_ref, tail_ref,
                xs_ref, wg_ref, wu_ref, wd_ref, bg_ref, bu_ref, bd_ref,
                ys_ref,
                xbuf, stage, yacc, wgu_bf, wd_bf, sem_in, sem_out):
    del item_e_ref
    i = pl.program_id(0)
    j = pl.program_id(1)
    n_items = pl.num_programs(0)
    n_chunks = pl.num_programs(1)
    nsub = item_n_ref[i]
    row0 = item_row_ref[i]
    tm, tf = MOE_TM, MOE_TF

    def in_copy(r):
        row = jnp.minimum(row0 + r * tm, xs_ref.shape[0] - tm)
        src = xs_ref.at[pl.ds(pl.multiple_of(row, tm), tm)]
        return pltpu.make_async_copy(src, stage.at[r & 1], sem_in.at[r & 1])

    def out_copy(r):
        dst = ys_ref.at[pl.ds(pl.multiple_of(row0 + r * tm, tm), tm)]
        return pltpu.make_async_copy(yacc.at[pl.ds(pl.multiple_of(r * tm, tm), tm)], dst, sem_out.at[r])

    @pl.when((j == 0) & (nsub > 0))
    def _():
        in_copy(0).start()
        in_copy(1).start()

    def chunk(first, last):
        wgu_bf[:, :tf] = wg_ref[0].astype(BF16)
        wgu_bf[:, tf:] = wu_ref[0].astype(BF16)
        wd_bf[...] = wd_ref[0].astype(BF16)

        def activation(r):
            rows = pl.ds(pl.multiple_of(r * tm, tm), tm)
            if first:
                in_copy(r).wait()
                xb = stage[r & 1].astype(BF16)
                xbuf[rows, :] = xb
                in_copy(r + 2).start()
            else:
                xb = xbuf[rows, :]
            gu = jnp.dot(xb, wgu_bf[...], preferred_element_type=F32)
            gate = jnp.minimum(gu[:, :tf] + bg_ref[0], SWIGLU_LIMIT)
            up = jnp.clip(gu[:, tf:] + bu_ref[0], -SWIGLU_LIMIT, SWIGLU_LIMIT)
            return (gate * jax.nn.sigmoid(SWIGLU_ALPHA * gate) * (up + 1.0)).astype(BF16)

        def down(r, act):
            rows = pl.ds(pl.multiple_of(r * tm, tm), tm)
            part = jnp.dot(act, wd_bf[...], preferred_element_type=F32)
            if first:
                yacc[rows, :] = part + bd_ref[0]
            else:
                yacc[rows, :] = yacc[rows, :] + part
            if last:
                out_copy(r).start()

        def step(r, act):
            nxt = activation(r + 1)
            down(r, act)
            return nxt

        def step_pair(p, act):
            return step(2 * p + 1, step(2 * p, act))

        n_steps = nsub - 1
        act = lax.fori_loop(0, n_steps // 2, step_pair, activation(0))
        act = lax.cond(n_steps % 2 == 1, lambda a: step(n_steps - 1, a), lambda a: a, act)
        down(nsub - 1, act)

        if first:
            in_copy(0).wait()
            in_copy(1).wait()
        if last:
            def drain(r, c):
                out_copy(r).wait()
                return c
            lax.fori_loop(0, nsub, drain, 0)

    @pl.when((nsub > 0) & (j == 0))
    def _():
        chunk(True, False)

    @pl.when((nsub > 0) & (j > 0) & (j < n_chunks - 1))
    def _():
        chunk(False, False)

    @pl.when((nsub > 0) & (j == n_chunks - 1))
    def _():
        chunk(False, True)

    @pl.when((i == n_items - 1) & (j == n_chunks - 1))
    def _():
        yacc[0:tm, :] = jnp.zeros((tm, D_MODEL), F32)
        tail0 = tail_ref[0]
        n_tail = ys_ref.shape[0] // tm - tail0

        def tail_copy(r):
            dst = ys_ref.at[pl.ds(pl.multiple_of((tail0 + r) * tm, tm), tm)]
            return pltpu.make_async_copy(yacc.at[0:tm], dst, sem_out.at[0])

        def start(r, c):
            tail_copy(r).start()
            return c

        def drain(r, c):
            tail_copy(r).wait()
            return c

        lax.fori_loop(0, n_tail, start, 0)
        lax.fori_loop(0, n_tail, drain, 0)


def _moe_call(item_e, item_row, item_n, tail_tile, xs, wg, wu, wd, bg, bu, bd):
    n_slots, d = xs.shape
    e, _, f = wg.shape
    n_items = item_e.shape[0]
    n_chunks = f // MOE_TF
    rows = MOE_SUB * MOE_TM

    def jeff(i, j, n_ref):
        return jnp.where(n_ref[i] > 0, j, n_chunks - 1)

    vmem = (rows * d * 2 + 2 * MOE_TM * d * 4 + rows * d * 4 + 3 * 2 * d * MOE_TF * 4
            + 3 * d * MOE_TF * 2 + 6 * 1024 * 1024)
    grid_spec = pltpu.PrefetchScalarGridSpec(
        num_scalar_prefetch=4,
        grid=(n_items, n_chunks),
        in_specs=[
            pl.BlockSpec(memory_space=pl.ANY),
            pl.BlockSpec((1, d, MOE_TF), lambda i, j, e_r, r_r, n_r, t_r: (e_r[i], 0, jeff(i, j, n_r))),
            pl.BlockSpec((1, d, MOE_TF), lambda i, j, e_r, r_r, n_r, t_r: (e_r[i], 0, jeff(i, j, n_r))),
            pl.BlockSpec((1, MOE_TF, d), lambda i, j, e_r, r_r, n_r, t_r: (e_r[i], jeff(i, j, n_r), 0)),
            pl.BlockSpec((1, 1, MOE_TF), lambda i, j, e_r, r_r, n_r, t_r: (e_r[i], 0, jeff(i, j, n_r))),
            pl.BlockSpec((1, 1, MOE_TF), lambda i, j, e_r, r_r, n_r, t_r: (e_r[i], 0, jeff(i, j, n_r))),
            pl.BlockSpec((1, 1, d), lambda i, j, e_r, r_r, n_r, t_r: (e_r[i], 0, 0)),
        ],
        out_specs=pl.BlockSpec(memory_space=pl.ANY),
        scratch_shapes=[
            pltpu.VMEM((rows, d), BF16),
            pltpu.VMEM((2, MOE_TM, d), F32),
            pltpu.VMEM((rows, d), F32),
            pltpu.VMEM((d, 2 * MOE_TF), BF16),
            pltpu.VMEM((MOE_TF, d), BF16),
            pltpu.SemaphoreType.DMA((2,)),
            pltpu.SemaphoreType.DMA((MOE_SUB,)),
        ],
    )
    return pl.pallas_call(
        _moe_kernel,
        grid_spec=grid_spec,
        out_shape=jax.ShapeDtypeStruct((n_slots, d), F32),
        compiler_params=pltpu.CompilerParams(
            dimension_semantics=("arbitrary", "arbitrary"), vmem_limit_bytes=_vmem_limit(vmem)),
        name="moe",
    )(item_e, item_row, item_n, tail_tile, xs, wg, wu, wd, bg.reshape(e, 1, f), bu.reshape(e, 1, f),
      bd.reshape(e, 1, d))


def _combine_kernel(pos_ref, posn_ref, ys_ref, x1_ref, tw_ref, g2_ref, l2g_ref, l2b_ref,
                    o_ref, buf, sem):
    i = pl.program_id(0)
    n = pl.num_programs(0)
    tm = x1_ref.shape[0]
    slot = i % 2

    def row_copy(p_ref, t, kk, s):
        p = p_ref[0, 0, t * TOP_K + kk]
        return pltpu.make_async_copy(ys_ref.at[pl.ds(p, 1)], buf.at[s, kk, pl.ds(t, 1)], sem.at[s])

    def issue(p_ref, s):
        def body(t, c):
            for kk in range(TOP_K):
                row_copy(p_ref, t, kk, s).start()
            return c
        lax.fori_loop(0, tm, body, 0)

    @pl.when(i == 0)
    def _():
        issue(pos_ref, 0)

    @pl.when(i + 1 < n)
    def _():
        issue(posn_ref, 1 - slot)

    def drain(t, c):
        for kk in range(TOP_K):
            row_copy(pos_ref, t, kk, slot).wait()
        return c
    lax.fori_loop(0, tm, drain, 0)

    tw = tw_ref[...]
    y = None
    for kk in range(TOP_K):
        part = tw[:, kk:kk + 1] * buf[slot, kk]
        y = part if y is None else y + part
    r = DEEPNORM_ALPHA * x1_ref[...] + g2_ref[0] * y
    o_ref[...] = _layer_norm(r) * l2g_ref[...] + l2b_ref[...]


def _combine_call(pos, ys, x1, tw, g2, l2g, l2b, seq):
    t, d = x1.shape
    tm = COMBINE_TM
    n = t // tm
    tps = seq // tm
    pos3 = pos.reshape(n, 1, tm * TOP_K)
    lanes = V7X_LANES
    vmem = 2 * TOP_K * tm * d * 4 + 4 * tm * d * 4 + 8 * 1024 * 1024
    return pl.pallas_call(
        _combine_kernel,
        grid=(n,),
        in_specs=[
            pl.BlockSpec((1, 1, tm * TOP_K), lambda i: (i, 0, 0), memory_space=pltpu.SMEM),
            pl.BlockSpec((1, 1, tm * TOP_K), lambda i: (jnp.minimum(i + 1, n - 1), 0, 0),
                         memory_space=pltpu.SMEM),
            pl.BlockSpec(memory_space=pl.ANY),
            pl.BlockSpec((tm, d), lambda i: (i, 0)),
            pl.BlockSpec((tm, lanes), lambda i: (i, 0)),
            pl.BlockSpec((1, 1, d), lambda i: (i // tps, 0, 0)),
            pl.BlockSpec((1, d), lambda i: (0, 0)),
            pl.BlockSpec((1, d), lambda i: (0, 0)),
        ],
        out_specs=pl.BlockSpec((tm, d), lambda i: (i, 0)),
        out_shape=jax.ShapeDtypeStruct((t, d), F32),
        scratch_shapes=[
            pltpu.VMEM((2, TOP_K, tm, d), F32),
            pltpu.SemaphoreType.DMA((2,)),
        ],
        compiler_params=pltpu.CompilerParams(
            dimension_semantics=("arbitrary",), vmem_limit_bytes=_vmem_limit(vmem)),
        name="combine",
    )(pos3, pos3, ys, x1, tw, g2, l2g, l2b)


def _routing_tables(counts, idx4, rank4, n_items):
    e = counts.shape[0]
    padded = (counts + MOE_TM - 1) // MOE_TM * MOE_TM
    pad_end = jnp.cumsum(padded)
    pad_start = pad_end - padded
    pos = pad_start[idx4] + rank4

    nsub_e = padded // MOE_TM
    items_e = (nsub_e + MOE_SUB - 1) // MOE_SUB
    item_end = jnp.cumsum(items_e)
    item_start = item_end - items_e
    total = item_end[-1]
    w = jnp.arange(n_items, dtype=I32)
    last = jnp.maximum(total - 1, 0)
    wv = jnp.minimum(w, last)
    ew = jnp.minimum(jnp.sum((item_end[None, :] <= wv[:, None]).astype(I32), axis=1), e - 1)
    s = wv - item_start[ew]
    valid = w < total
    item_row = jnp.where(valid, pad_start[ew] + s * (MOE_SUB * MOE_TM), 0).astype(I32)
    item_n = jnp.where(valid, jnp.clip(nsub_e[ew] - s * MOE_SUB, 0, MOE_SUB), 0).astype(I32)
    tail_tile = (pad_end[-1:] // MOE_TM).astype(I32)
    group_tbl = jnp.stack([counts, pad_start]).astype(I32)
    return pos.astype(I32), ew, item_row, item_n, tail_tile, group_tbl


def kernel(x, c, w_ada, b_ada, w_in, w_gk, b_gk, w_pool, b_pool, pool_scale, gla_norm_w, w_out,
           ln1_g, ln1_b, w_router, b_router, w_gate, b_gate, w_up, b_up, w_down, b_down,
           ln2_g, ln2_b):
    batch, seq, d = x.shape
    t = batch * seq
    e = w_router.shape[-1]
    lanes = V7X_LANES
    for l in range(w_ada.shape[0]):
        mod = _mod_call(c, w_ada[l], b_ada[l])
        sh1, sc1, g1, sh2, sc2, g2 = [m.reshape(batch, 1, d) for m in jnp.split(mod, 6, axis=-1)]

        wi = w_in[l]
        lr0 = POOL_WIDTH + 2 * GLA_K_WIDTH + GLA_V_WIDTH
        w_re = jnp.concatenate(
            [wi[:, :lr0], wi[:, lr0 + GLA_GATE_RANK:], wi[:, lr0:lr0 + GLA_GATE_RANK],
             jnp.zeros((d, lanes - GLA_GATE_RANK), wi.dtype)], axis=1).astype(BF16)
        x2 = x.reshape(t, d)
        proj = _inproj_call(x2, sh1, sc1, w_re, seq)

        wgk_pad = jnp.concatenate(
            [w_gk[l], jnp.zeros((lanes - GLA_GATE_RANK, GLA_K_WIDTH), w_gk.dtype)], axis=0).astype(BF16)
        y_gla = _gla_call(proj, wgk_pad, b_gk[l].reshape(1, -1), gla_norm_w[l].reshape(1, -1), batch, seq)

        wr = w_router[l]
        wr_hi = wr.astype(BF16)
        wr_lo = (wr - wr_hi.astype(F32)).astype(BF16)
        wra = jnp.concatenate([wr_hi, wr_lo, jnp.zeros((d, lanes - 2 * e), BF16)], axis=1)
        wrb = jnp.concatenate([wr_hi, jnp.zeros((d, lanes - e), BF16)], axis=1)
        br = jnp.concatenate([b_router[l], jnp.zeros((lanes - e,), F32)]).reshape(1, lanes)
        x1, h2, idx, tw, rank, cnt = _outproj_call(
            proj, y_gla, x2, g1, sh2, sc2,
            w_pool[l].astype(BF16), b_pool[l].reshape(1, -1), pool_scale[l].reshape(1, -1),
            w_out[l].astype(BF16), ln1_g[l].reshape(1, -1), ln1_b[l].reshape(1, -1),
            wra, wrb, br, seq)

        n_slots = t * TOP_K + e * MOE_TM
        n_items = e + n_slots // (MOE_SUB * MOE_TM)
        counts = cnt[0, :e].astype(I32)
        pos, item_e, item_row, item_n, tail_tile, group_tbl = _routing_tables(
            counts, idx[:, :TOP_K], rank[:, :TOP_K], n_items)

        xs = _dispatch_call(group_tbl, pos, h2, n_slots)
        ys = _moe_call(item_e, item_row, item_n, tail_tile, xs, w_gate[l], w_up[l], w_down[l],
                       b_gate[l], b_up[l], b_down[l])
        out = _combine_call(pos, ys, x1, tw, g2, ln2_g[l].reshape(1, -1), ln2_b[l].reshape(1, -1), seq)
        x = out.reshape(batch, seq, d)
    return x
```

```python
import functools

import jax
import jax.numpy as jnp
from jax import lax
from jax.experimental import pallas as pl
from jax.experimental.pallas import tpu as pltpu

F32 = jnp.float32
BF16 = jnp.bfloat16
I32 = jnp.int32

D_MODEL = 2048
POOL_WIDTH = 1024
POOL_GROUPS = 4
POOL_GROUP_DIM = 256
POOL_WINDOWS = (2, 4, 8, 16)
POOL_HALO = 16
GLA_HEADS = 4
GLA_HEAD_K = 128
GLA_HEAD_V = 256
GLA_K_WIDTH = GLA_HEADS * GLA_HEAD_K
GLA_V_WIDTH = GLA_HEADS * GLA_HEAD_V
GLA_GATE_RANK = 16
GLA_GATE_TAU = 16.0
GLA_CHUNK = 64
GLA_NORM_EPS = 1e-6
N_EXPERTS = 32
TOP_K = 4
D_FF = 2048
SWIGLU_ALPHA = 1.702
SWIGLU_LIMIT = 7.0
DEPTH = 1
DEEPNORM_ALPHA = (2.0 * DEPTH) ** 0.25
LN_EPS = 1e-5

V7X_LANES = 128
V7X_SUBLANES = 8
V7X_VMEM_BYTES = 64 * 1024 * 1024

MOD_TN = 512
MOD_KC = 64
INPROJ_TM = 512
INPROJ_TN = 512
PROJ_WIDTH = 4096 + V7X_LANES
GLA_BLOCK = 256
OUT_TM = 512
OUT_PARTS = 1
DISPATCH_TM = 256
MOE_TM = 256
MOE_SUB = 9
MOE_TF = 256
COMBINE_TM = 128


def _vmem_limit(nbytes):
    return int(min(nbytes, V7X_VMEM_BYTES - 4 * 1024 * 1024))


def _layer_norm(x):
    mu = jnp.mean(x, axis=-1, keepdims=True)
    xc = x - mu
    var = jnp.mean(xc * xc, axis=-1, keepdims=True)
    return xc * lax.rsqrt(var + LN_EPS)


def _mod_kernel(c_ref, w_ref, b_ref, o_ref):
    nb = c_ref.shape[0]
    d, tn = w_ref.shape
    rep = tn // V7X_LANES
    g = MOD_KC // V7X_SUBLANES

    def body(i, accs):
        k0 = pl.multiple_of(i * MOD_KC, MOD_KC)
        w = w_ref[pl.ds(k0, MOD_KC), :].reshape(g, V7X_SUBLANES, tn)
        out = []
        for b in range(nb):
            cb = c_ref[b, pl.ds(k0, MOD_KC), :]
            s = cb * jax.nn.sigmoid(cb)
            s = jnp.concatenate([s] * rep, axis=1).reshape(g, V7X_SUBLANES, tn)
            out.append(accs[b] + jnp.sum(w * s, axis=0))
        return tuple(out)

    init = tuple(jnp.zeros((V7X_SUBLANES, tn), F32) for _ in range(nb))
    accs = lax.fori_loop(0, d // MOD_KC, body, init)
    rows = [jnp.sum(a, axis=0, keepdims=True) for a in accs]
    rows.append(jnp.zeros((V7X_SUBLANES - nb, tn), F32))
    o_ref[...] = jnp.concatenate(rows, axis=0) + b_ref[...]


def _mod_call(c, w_ada, b_ada):
    nb, d = c.shape
    n = w_ada.shape[1]
    c_rep = jnp.broadcast_to(c[:, :, None], (nb, d, V7X_LANES))
    out = pl.pallas_call(
        _mod_kernel,
        grid=(n // MOD_TN,),
        in_specs=[
            pl.BlockSpec((nb, d, V7X_LANES), lambda j: (0, 0, 0)),
            pl.BlockSpec((d, MOD_TN), lambda j: (0, j)),
            pl.BlockSpec((1, MOD_TN), lambda j: (0, j)),
        ],
        out_specs=pl.BlockSpec((V7X_SUBLANES, MOD_TN), lambda j: (0, j)),
        out_shape=jax.ShapeDtypeStruct((V7X_SUBLANES, n), F32),
        compiler_params=pltpu.CompilerParams(dimension_semantics=("arbitrary",)),
        name="mod",
    )(c_rep, w_ada, b_ada.reshape(1, n))
    return out[:nb]


def _inproj_kernel(x_ref, sh_ref, sc_ref, w_ref, o_ref):
    h = _layer_norm(x_ref[...]) * (1.0 + sc_ref[0]) + sh_ref[0]
    hb = h.astype(BF16)
    n = w_ref.shape[1]
    for n0 in range(0, n, INPROJ_TN):
        n1 = min(n0 + INPROJ_TN, n)
        o_ref[:, n0:n1] = jnp.dot(hb, w_ref[:, n0:n1], preferred_element_type=F32)


def _inproj_call(x2, sh1, sc1, w_re, seq):
    t, d = x2.shape
    n = w_re.shape[1]
    tpb = seq // INPROJ_TM
    vmem = 2 * INPROJ_TM * d * 4 + 2 * INPROJ_TM * n * 4 + d * n * 2 + 8 * 1024 * 1024
    return pl.pallas_call(
        _inproj_kernel,
        grid=(t // INPROJ_TM,),
        in_specs=[
            pl.BlockSpec((INPROJ_TM, d), lambda i: (i, 0)),
            pl.BlockSpec((1, 1, d), lambda i: (i // tpb, 0, 0)),
            pl.BlockSpec((1, 1, d), lambda i: (i // tpb, 0, 0)),
            pl.BlockSpec((d, n), lambda i: (0, 0), pipeline_mode=pl.Buffered(1)),
        ],
        out_specs=pl.BlockSpec((INPROJ_TM, n), lambda i: (i, 0)),
        out_shape=jax.ShapeDtypeStruct((t, n), F32),
        compiler_params=pltpu.CompilerParams(
            dimension_semantics=("arbitrary",), vmem_limit_bytes=_vmem_limit(vmem)),
        name="inproj",
    )(x2, sh1, sc1, w_re)


def _chunk_cumsum(x, row_in_chunk):
    s = 1
    while s < GLA_CHUNK:
        x = x + jnp.where(row_in_chunk >= s, pltpu.roll(x, s, 0), 0.0)
        s *= 2
    return x


def _group_ref(b, m, row):
    n, dk = b.shape
    g = 2 * m
    if g >= V7X_SUBLANES:
        r = b.reshape(n // g, g, dk)[:, m - 1:m, :]
        return jnp.broadcast_to(r, (n // g, g, dk)).reshape(n, dk)
    rr = row & (g - 1)
    out = None
    for r in range(g):
        sh = (r - (m - 1)) % n
        shifted = b if sh == 0 else pltpu.roll(b, sh, 0)
        out = shifted if out is None else jnp.where(rr == r, shifted, out)
    return out


def _dot_nt(a, b):
    return lax.dot_general(a, b, (((1,), (1,)), ((), ())), preferred_element_type=F32)


def _gla_kernel(q_ref, k_ref, v_ref, g_ref, lr_ref, wgk_ref, bgk_ref, nw_ref, o_ref, state_ref):
    n = q_ref.shape[0]
    dk, dv = GLA_HEAD_K, GLA_HEAD_V
    n_chunks = n // GLA_CHUNK

    @pl.when(pl.program_id(1) == 0)
    def _():
        state_ref[...] = jnp.zeros_like(state_ref)

    row = lax.broadcasted_iota(I32, (n, dk), 0)
    row_in_chunk = row & (GLA_CHUNK - 1)
    ti = lax.broadcasted_iota(I32, (n, n), 0)
    si = lax.broadcasted_iota(I32, (n, n), 1)
    x = ti ^ si
    level = jnp.zeros((n, n), I32)
    m = 1
    while m < GLA_CHUNK:
        level = level + (x >= m).astype(I32)
        m *= 2
    level = jnp.where((si <= ti) & (x < GLA_CHUNK), level, -1)

    z = jnp.dot(lr_ref[...].astype(BF16), wgk_ref[...], preferred_element_type=F32) + bgk_ref[...]
    log_alpha = (jnp.minimum(z, 0.0) - jnp.log(1.0 + jnp.exp(-jnp.abs(z)))) * (1.0 / GLA_GATE_TAU)

    scale = GLA_HEAD_K ** -0.5
    for h in range(GLA_HEADS):
        q = q_ref[:, h * dk:(h + 1) * dk] * scale
        k = k_ref[:, h * dk:(h + 1) * dk]
        vb = v_ref[:, h * dv:(h + 1) * dv].astype(BF16)
        bc = _chunk_cumsum(log_alpha[:, h * dk:(h + 1) * dk], row_in_chunk)

        scores = jnp.where(level == 0, _dot_nt(q.astype(BF16), k.astype(BF16)), 0.0)
        m = 1
        lvl = 1
        while m < GLA_CHUNK:
            e = jnp.exp(-jnp.abs(bc - _group_ref(bc, m, row)))
            s = _dot_nt((q * e).astype(BF16), (k * e).astype(BF16))
            scores = jnp.where(level == lvl, s, scores)
            m *= 2
            lvl += 1
        o = jnp.dot(scores.astype(BF16), vb, preferred_element_type=F32)

        qd = (q * jnp.exp(bc)).astype(BF16)
        inter = []
        for c in range(n_chunks):
            r0 = c * GLA_CHUNK
            r1 = r0 + GLA_CHUNK
            state = state_ref[h]
            inter.append(jnp.dot(qd[r0:r1], state.astype(BF16), preferred_element_type=F32))
            bcc = bc[r0:r1]
            blast = bcc[GLA_CHUNK - 1:GLA_CHUNK, :]
            kd_t = (k[r0:r1] * jnp.exp(blast - bcc)).T.astype(BF16)
            decay = jnp.exp(bcc.T[:, GLA_CHUNK - 1:GLA_CHUNK])
            state_ref[h] = decay * state + jnp.dot(kd_t, vb[r0:r1], preferred_element_type=F32)
        o = o + jnp.concatenate(inter, axis=0)

        o = o * lax.rsqrt(jnp.mean(o * o, axis=-1, keepdims=True) + GLA_NORM_EPS) * nw_ref[...]
        g = g_ref[:, h * dv:(h + 1) * dv]
        o_ref[:, h * dv:(h + 1) * dv] = (o * (g * jax.nn.sigmoid(g))).astype(o_ref.dtype)


def _gla_call(proj, wgk_pad, bgk, norm_w, batch, seq):
    t = proj.shape[0]
    nb = seq // GLA_BLOCK
    kw, vw = GLA_K_WIDTH, GLA_V_WIDTH
    q_col = POOL_WIDTH // kw
    v_col = (POOL_WIDTH + 2 * kw) // vw
    g_col = v_col + 1
    lr_col = (POOL_WIDTH + 2 * kw + 2 * vw) // V7X_LANES
    row = lambda b, j: b * nb + j
    return pl.pallas_call(
        _gla_kernel,
        grid=(batch, nb),
        in_specs=[
            pl.BlockSpec((GLA_BLOCK, kw), lambda b, j: (row(b, j), q_col)),
            pl.BlockSpec((GLA_BLOCK, kw), lambda b, j: (row(b, j), q_col + 1)),
            pl.BlockSpec((GLA_BLOCK, vw), lambda b, j: (row(b, j), v_col)),
            pl.BlockSpec((GLA_BLOCK, vw), lambda b, j: (row(b, j), g_col)),
            pl.BlockSpec((GLA_BLOCK, V7X_LANES), lambda b, j: (row(b, j), lr_col)),
            pl.BlockSpec((V7X_LANES, kw), lambda b, j: (0, 0)),
            pl.BlockSpec((1, kw), lambda b, j: (0, 0)),
            pl.BlockSpec((1, GLA_HEAD_V), lambda b, j: (0, 0)),
        ],
        out_specs=pl.BlockSpec((GLA_BLOCK, vw), lambda b, j: (row(b, j), 0)),
        out_shape=jax.ShapeDtypeStruct((t, vw), BF16),
        scratch_shapes=[pltpu.VMEM((GLA_HEADS, GLA_HEAD_K, GLA_HEAD_V), F32)],
        compiler_params=pltpu.CompilerParams(dimension_semantics=("arbitrary", "arbitrary")),
        name="gla",
    )(proj, proj, proj, proj, proj, wgk_pad, bgk, norm_w)


def _outproj_kernel(u_ref, yg_ref, x_ref, g1_ref, sh2_ref, sc2_ref, wp_ref, bp_ref, ps_ref, wo_ref,
                    l1g_ref, l1b_ref, wra_ref, wrb_ref, br_ref,
                    x1_ref, h2_ref, idx_ref, tw_ref, rank_ref, cnt_ref,
                    halo_ref, carry_ref, *, tiles_per_seq):
    i = pl.program_id(0)
    tm = u_ref.shape[0]
    seq_tile = i % tiles_per_seq

    @pl.when(seq_tile == 0)
    def _():
        halo_ref[...] = jnp.zeros_like(halo_ref)

    @pl.when(i == 0)
    def _():
        carry_ref[...] = jnp.zeros_like(carry_ref)

    gd = POOL_GROUP_DIM
    part_rows = tm // OUT_PARTS
    lane = lax.broadcasted_iota(I32, (part_rows, V7X_LANES), 1)
    neg_inf = jnp.float32(-jnp.inf)
    tri = (lax.broadcasted_iota(I32, (part_rows, part_rows), 1)
           < lax.broadcasted_iota(I32, (part_rows, part_rows), 0)).astype(BF16)

    def part(r0, halo, carry):
        rows = slice(r0, r0 + part_rows)
        u = u_ref[rows, :]
        pos1 = seq_tile * tm + r0 + lax.broadcasted_iota(I32, (part_rows, gd), 0) + 1
        mixed = []
        for gi, win in enumerate(POOL_WINDOWS):
            cols = slice(gi * gd, (gi + 1) * gd)
            ug = u[:, cols]
            ws = jnp.concatenate([halo[:, cols], ug], axis=0)
            s = 1
            while s < win:
                ws = ws + pltpu.roll(ws, s, 0)
                s *= 2
            count = jnp.minimum(pos1, win).astype(F32)
            pooled = ws[POOL_HALO:, :] / count - ug
            m = jnp.dot(pooled.astype(BF16), wp_ref[gi], preferred_element_type=F32)
            mixed.append(((m + bp_ref[:, cols]) * ps_ref[:, cols]).astype(BF16))
        cat = jnp.concatenate(mixed + [yg_ref[rows, :]], axis=1)
        y = jnp.dot(cat, wo_ref[...], preferred_element_type=F32)

        x1 = _layer_norm(DEEPNORM_ALPHA * x_ref[rows, :] + g1_ref[0] * y) * l1g_ref[...] + l1b_ref[...]
        x1_ref[rows, :] = x1
        h2 = _layer_norm(x1) * (1.0 + sc2_ref[0]) + sh2_ref[0]
        h2_ref[rows, :] = h2

        hb = h2.astype(BF16)
        h_lo = (h2 - hb.astype(F32)).astype(BF16)
        ra = jnp.dot(hb, wra_ref[...], preferred_element_type=F32)
        rb = jnp.dot(h_lo, wrb_ref[...], preferred_element_type=F32)
        logits = ra + pltpu.roll(ra, V7X_LANES - N_EXPERTS, 1) + rb + br_ref[...]
        work = jnp.where(lane < N_EXPERTS, logits, neg_inf)
        vals, idxs = [], []
        for _ in range(TOP_K):
            mx = jnp.max(work, axis=-1, keepdims=True)
            sel = jnp.min(jnp.where(work == mx, lane, V7X_LANES), axis=-1, keepdims=True)
            vals.append(mx)
            idxs.append(sel)
            work = jnp.where(lane == sel, neg_inf, work)
        exps = [jnp.exp(v - vals[0]) for v in vals]
        denom = exps[0] + exps[1] + exps[2] + exps[3]

        chosen = jnp.where((lane < N_EXPERTS) & (work == neg_inf), 1.0, 0.0)
        excl = jnp.dot(tri, chosen.astype(BF16), preferred_element_type=F32) + carry
        idx_out = jnp.zeros((part_rows, V7X_LANES), I32)
        tw_out = jnp.zeros((part_rows, V7X_LANES), F32)
        rank_out = jnp.zeros((part_rows, V7X_LANES), F32)
        for kk in range(TOP_K):
            rk = jnp.sum(jnp.where(lane == idxs[kk], excl, 0.0), axis=-1, keepdims=True)
            idx_out = jnp.where(lane == kk, idxs[kk], idx_out)
            tw_out = jnp.where(lane == kk, exps[kk] / denom, tw_out)
            rank_out = jnp.where(lane == kk, rk, rank_out)
        idx_ref[rows, :] = idx_out
        tw_ref[rows, :] = tw_out
        rank_ref[rows, :] = rank_out.astype(I32)
        return carry + jnp.sum(chosen, axis=0, keepdims=True)

    carry = carry_ref[0:1, :]
    halo = halo_ref[...]
    for p in range(OUT_PARTS):
        r0 = p * part_rows
        carry = part(r0, halo, carry)
        halo = u_ref[r0 + part_rows - POOL_HALO:r0 + part_rows, :]
    halo_ref[...] = halo
    carry_ref[...] = jnp.broadcast_to(carry, carry_ref.shape)
    cnt_ref[...] = jnp.broadcast_to(carry, cnt_ref.shape)


def _outproj_call(proj, y_gla, x2, g1, sh2, sc2, wp, bp, ps, wo, l1g, l1b, wra, wrb, br, seq):
    t, d = x2.shape
    tm = OUT_TM
    tps = seq // tm
    bidx = lambda i: (i // tps, 0, 0)
    const2 = lambda i: (0, 0)
    row = lambda i: (i, 0)
    lanes = V7X_LANES
    vmem = (2 * tm * (POOL_WIDTH * 4 + GLA_V_WIDTH * 2 + d * 4 + d * 4 + d * 4 + 3 * lanes * 4)
            + d * d * 2 + 16 * 1024 * 1024)
    return pl.pallas_call(
        functools.partial(_outproj_kernel, tiles_per_seq=tps),
        grid=(t // tm,),
        in_specs=[
            pl.BlockSpec((tm, POOL_WIDTH), row),
            pl.BlockSpec((tm, GLA_V_WIDTH), row),
            pl.BlockSpec((tm, d), row),
            pl.BlockSpec((1, 1, d), bidx),
            pl.BlockSpec((1, 1, d), bidx),
            pl.BlockSpec((1, 1, d), bidx),
            pl.BlockSpec((POOL_GROUPS, POOL_GROUP_DIM, POOL_GROUP_DIM), lambda i: (0, 0, 0)),
            pl.BlockSpec((1, POOL_WIDTH), const2),
            pl.BlockSpec((1, POOL_WIDTH), const2),
            pl.BlockSpec((d, d), const2, pipeline_mode=pl.Buffered(1)),
            pl.BlockSpec((1, d), const2),
            pl.BlockSpec((1, d), const2),
            pl.BlockSpec((d, lanes), const2),
            pl.BlockSpec((d, lanes), const2),
            pl.BlockSpec((1, lanes), const2),
        ],
        out_specs=[
            pl.BlockSpec((tm, d), row),
            pl.BlockSpec((tm, d), row),
            pl.BlockSpec((tm, lanes), row),
            pl.BlockSpec((tm, lanes), row),
            pl.BlockSpec((tm, lanes), row),
            pl.BlockSpec((V7X_SUBLANES, lanes), const2),
        ],
        out_shape=[
            jax.ShapeDtypeStruct((t, d), F32),
            jax.ShapeDtypeStruct((t, d), F32),
            jax.ShapeDtypeStruct((t, lanes), I32),
            jax.ShapeDtypeStruct((t, lanes), F32),
            jax.ShapeDtypeStruct((t, lanes), I32),
            jax.ShapeDtypeStruct((V7X_SUBLANES, lanes), F32),
        ],
        scratch_shapes=[
            pltpu.VMEM((POOL_HALO, POOL_WIDTH), F32),
            pltpu.VMEM((V7X_SUBLANES, lanes), F32),
        ],
        compiler_params=pltpu.CompilerParams(
            dimension_semantics=("arbitrary",), vmem_limit_bytes=_vmem_limit(vmem)),
        name="outproj",
    )(proj, y_gla, x2, g1, sh2, sc2, wp, bp, ps, wo, l1g, l1b, wra, wrb, br)


def _dispatch_kernel(tbl_ref, pos_ref, h_ref, xs_ref, zero_ref, sem, zsem):
    tm = h_ref.shape[0]
    n_experts = tbl_ref.shape[1]
    n_tiles = xs_ref.shape[0] // MOE_TM

    def pad_copy(p):
        return pltpu.make_async_copy(zero_ref.at[pl.ds(0, 1)], xs_ref.at[pl.ds(p, 1)], zsem)

    def tail_copy(tile):
        dst = xs_ref.at[pl.ds(pl.multiple_of(tile * MOE_TM, MOE_TM), MOE_TM)]
        return pltpu.make_async_copy(zero_ref, dst, zsem)

    def for_each_pad(fn):
        def per_expert(e, c):
            cnt = tbl_ref[0, e]
            first = tbl_ref[1, e] + cnt
            n_pad = (cnt + MOE_TM - 1) // MOE_TM * MOE_TM - cnt

            def per_row(r, c2):
                fn(pad_copy(first + r))
                return c2
            lax.fori_loop(0, n_pad, per_row, 0)
            return c
        lax.fori_loop(0, n_experts, per_expert, 0)
        last = n_experts - 1
        tail0 = (tbl_ref[1, last] + tbl_ref[0, last] + MOE_TM - 1) // MOE_TM

        def per_tile(r, c):
            fn(tail_copy(tail0 + r))
            return c
        lax.fori_loop(0, n_tiles - tail0, per_tile, 0)

    @pl.when(pl.program_id(0) == 0)
    def _():
        zero_ref[...] = jnp.zeros_like(zero_ref)
        for_each_pad(lambda cp: cp.start())
        for_each_pad(lambda cp: cp.wait())

    def row_copy(t, p):
        return pltpu.make_async_copy(h_ref.at[pl.ds(t, 1)], xs_ref.at[pl.ds(p, 1)], sem)

    def issue(t, carry):
        for kk in range(TOP_K):
            row_copy(t, pos_ref[0, 0, t * TOP_K + kk]).start(priority=kk % 2)
        return carry

    lax.fori_loop(0, tm, issue, 0, unroll=2)
    for kk in range(TOP_K):
        pltpu.make_async_copy(h_ref, xs_ref.at[pl.ds(0, tm)], sem).wait()


def _dispatch_call(tbl, pos, h2, n_slots):
    t, w = h2.shape
    tm = DISPATCH_TM
    pos3 = pos.reshape(t // tm, 1, tm * TOP_K)
    return pl.pallas_call(
        _dispatch_kernel,
        grid=(t // tm,),
        in_specs=[
            pl.BlockSpec(memory_space=pltpu.SMEM),
            pl.BlockSpec((1, 1, tm * TOP_K), lambda i: (i, 0, 0), memory_space=pltpu.SMEM),
            pl.BlockSpec((tm, w), lambda i: (i, 0)),
        ],
        out_specs=pl.BlockSpec(memory_space=pl.ANY),
        out_shape=jax.ShapeDtypeStruct((n_slots, w), h2.dtype),
        scratch_shapes=[
            pltpu.VMEM((MOE_TM, w), h2.dtype),
            pltpu.SemaphoreType.DMA(()),
            pltpu.SemaphoreType.DMA(()),
        ],
        compiler_params=pltpu.CompilerParams(dimension_semantics=("arbitrary",)),
        name="dispatch",
    )(tbl, pos3, h2)


def _moe_kernel(item_e_ref, item_row_ref, item_n_ref, tail_ref,
                xs_ref, wg_ref, wu_ref, wd_ref, bg_ref, bu_ref, bd_ref,
                ys_ref,
                xbuf, stage, yacc, wgu_bf, wd_bf, sem_in, sem_out):
    del item_e_ref
    i = pl.program_id(0)
    j = pl.program_id(1)
    n_items = pl.num_programs(0)
    n_chunks = pl.num_programs(1)
    nsub = item_n_ref[i]
    row0 = item_row_ref[i]
    tm, tf = MOE_TM, MOE_TF

    def in_copy(r):
        row = jnp.minimum(row0 + r * tm, xs_ref.shape[0] - tm)
        src = xs_ref.at[pl.ds(pl.multiple_of(row, tm), tm)]
        return pltpu.make_async_copy(src, stage.at[r & 1], sem_in.at[r & 1])

    def out_copy(r):
        dst = ys_ref.at[pl.ds(pl.multiple_of(row0 + r * tm, tm), tm)]
        return pltpu.make_async_copy(yacc.at[pl.ds(pl.multiple_of(r * tm, tm), tm)], dst, sem_out.at[r])

    @pl.when((j == 0) & (nsub > 0))
    def _():
        in_copy(0).start()
        in_copy(1).start()

    def chunk(first, last):
        wgu_bf[:, :tf] = wg_ref[0].astype(BF16)
        wgu_bf[:, tf:] = wu_ref[0].astype(BF16)
        wd_bf[...] = wd_ref[0].astype(BF16)

        def activation(r):
            rows = pl.ds(pl.multiple_of(r * tm, tm), tm)
            if first:
                in_copy(r).wait()
                xb = stage[r & 1].astype(BF16)
                xbuf[rows, :] = xb
                in_copy(r + 2).start()
            else:
                xb = xbuf[rows, :]
            gu = jnp.dot(xb, wgu_bf[...], preferred_element_type=F32)
            gate = jnp.minimum(gu[:, :tf] + bg_ref[0], SWIGLU_LIMIT)
            up = jnp.clip(gu[:, tf:] + bu_ref[0], -SWIGLU_LIMIT, SWIGLU_LIMIT)
            return (gate * jax.nn.sigmoid(SWIGLU_ALPHA * gate) * (up + 1.0)).astype(BF16)

        def down(r, act):
            rows = pl.ds(pl.multiple_of(r * tm, tm), tm)
            part = jnp.dot(act, wd_bf[...], preferred_element_type=F32)
            if first:
                yacc[rows, :] = part + bd_ref[0]
            else:
                yacc[rows, :] = yacc[rows, :] + part
            if last:
                out_copy(r).start()

        def step(r, act):
            nxt = activation(r + 1)
            down(r, act)
            return nxt

        def step_pair(p, act):
            return step(2 * p + 1, step(2 * p, act))

        n_steps = nsub - 1
        act = lax.fori_loop(0, n_steps // 2, step_pair, activation(0))
        act = lax.cond(n_steps % 2 == 1, lambda a: step(n_steps - 1, a), lambda a: a, act)
        down(nsub - 1, act)

        if first:
            in_copy(0).wait()
            in_copy(1).wait()
        if last:
            def drain(r, c):
                out_copy(r).wait()
                return c
            lax.fori_loop(0, nsub, drain, 0)

    @pl.when((nsub > 0) & (j == 0))
    def _():
        chunk(True, False)

    @pl.when((nsub > 0) & (j > 0) & (j < n_chunks - 1))
    def _():
        chunk(False, False)

    @pl.when((nsub > 0) & (j == n_chunks - 1))
    def _():
        chunk(False, True)

    @pl.when((i == n_items - 1) & (j == n_chunks - 1))
    def _():
        yacc[0:tm, :] = jnp.zeros((tm, D_MODEL), F32)
        tail0 = tail_ref[0]
        n_tail = ys_ref.shape[0] // tm - tail0

        def tail_copy(r):
            dst = ys_ref.at[pl.ds(pl.multiple_of((tail0 + r) * tm, tm), tm)]
            return pltpu.make_async_copy(yacc.at[0:tm], dst, sem_out.at[0])

        def start(r, c):
            tail_copy(r).start()
            return c

        def drain(r, c):
            tail_copy(r).wait()
            return c

        lax.fori_loop(0, n_tail, start, 0)
        lax.fori_loop(0, n_tail, drain, 0)


def _moe_call(item_e, item_row, item_n, tail_tile, xs, wg, wu, wd, bg, bu, bd):
    n_slots, d = xs.shape
    e, _, f = wg.shape
    n_items = item_e.shape[0]
    n_chunks = f // MOE_TF
    rows = MOE_SUB * MOE_TM

    def jeff(i, j, n_ref):
        return jnp.where(n_ref[i] > 0, j, n_chunks - 1)

    vmem = (rows * d * 2 + 2 * MOE_TM * d * 4 + rows * d * 4 + 3 * 2 * d * MOE_TF * 4
            + 3 * d * MOE_TF * 2 + 6 * 1024 * 1024)
    grid_spec = pltpu.PrefetchScalarGridSpec(
        num_scalar_prefetch=4,
        grid=(n_items, n_chunks),
        in_specs=[
            pl.BlockSpec(memory_space=pl.ANY),
            pl.BlockSpec((1, d, MOE_TF), lambda i, j, e_r, r_r, n_r, t_r: (e_r[i], 0, jeff(i, j, n_r))),
            pl.BlockSpec((1, d, MOE_TF), lambda i, j, e_r, r_r, n_r, t_r: (e_r[i], 0, jeff(i, j, n_r))),
            pl.BlockSpec((1, MOE_TF, d), lambda i, j, e_r, r_r, n_r, t_r: (e_r[i], jeff(i, j, n_r), 0)),
            pl.BlockSpec((1, 1, MOE_TF), lambda i, j, e_r, r_r, n_r, t_r: (e_r[i], 0, jeff(i, j, n_r))),
            pl.BlockSpec((1, 1, MOE_TF), lambda i, j, e_r, r_r, n_r, t_r: (e_r[i], 0, jeff(i, j, n_r))),
            pl.BlockSpec((1, 1, d), lambda i, j, e_r, r_r, n_r, t_r: (e_r[i], 0, 0)),
        ],
        out_specs=pl.BlockSpec(memory_space=pl.ANY),
        scratch_shapes=[
            pltpu.VMEM((rows, d), BF16),
            pltpu.VMEM((2, MOE_TM, d), F32),
            pltpu.VMEM((rows, d), F32),
            pltpu.VMEM((d, 2 * MOE_TF), BF16),
            pltpu.VMEM((MOE_TF, d), BF16),
            pltpu.SemaphoreType.DMA((2,)),
            pltpu.SemaphoreType.DMA((MOE_SUB,)),
        ],
    )
    return pl.pallas_call(
        _moe_kernel,
        grid_spec=grid_spec,
        out_shape=jax.ShapeDtypeStruct((n_slots, d), F32),
        compiler_params=pltpu.CompilerParams(
            dimension_semantics=("arbitrary", "arbitrary"), vmem_limit_bytes=_vmem_limit(vmem)),
        name="moe",
    )(item_e, item_row, item_n, tail_tile, xs, wg, wu, wd, bg.reshape(e, 1, f), bu.reshape(e, 1, f),
      bd.reshape(e, 1, d))


def _combine_kernel(pos_ref, posn_ref, ys_ref, x1_ref, tw_ref, g2_ref, l2g_ref, l2b_ref,
                    o_ref, buf, sem):
    i = pl.program_id(0)
    n = pl.num_programs(0)
    tm = x1_ref.shape[0]
    slot = i % 2

    def row_copy(p_ref, t, kk, s):
        p = p_ref[0, 0, t * TOP_K + kk]
        return pltpu.make_async_copy(ys_ref.at[pl.ds(p, 1)], buf.at[s, pl.ds(kk * tm + t, 1)], sem.at[s])

    def issue(p_ref, s):
        def body(t, c):
            for kk in range(TOP_K):
                row_copy(p_ref, t, kk, s).start(priority=kk % 2)
            return c
        lax.fori_loop(0, tm, body, 0, unroll=2)

    @pl.when(i == 0)
    def _():
        issue(pos_ref, 0)

    @pl.when(i + 1 < n)
    def _():
        issue(posn_ref, 1 - slot)

    pltpu.make_async_copy(ys_ref.at[pl.ds(0, TOP_K * tm)], buf.at[slot], sem.at[slot]).wait()

    tw = tw_ref[...]
    y = None
    for kk in range(TOP_K):
        part = tw[:, kk:kk + 1] * buf[slot, kk * tm:(kk + 1) * tm, :]
        y = part if y is None else y + part
    r = DEEPNORM_ALPHA * x1_ref[...] + g2_ref[0] * y
    o_ref[...] = _layer_norm(r) * l2g_ref[...] + l2b_ref[...]


def _combine_call(pos, ys, x1, tw, g2, l2g, l2b, seq):
    t, d = x1.shape
    tm = COMBINE_TM
    n = t // tm
    tps = seq // tm
    pos3 = pos.reshape(n, 1, tm * TOP_K)
    lanes = V7X_LANES
    vmem = 2 * TOP_K * tm * d * 4 + 4 * tm * d * 4 + 8 * 1024 * 1024
    return pl.pallas_call(
        _combine_kernel,
        grid=(n,),
        in_specs=[
            pl.BlockSpec((1, 1, tm * TOP_K), lambda i: (i, 0, 0), memory_space=pltpu.SMEM),
            pl.BlockSpec((1, 1, tm * TOP_K), lambda i: (jnp.minimum(i + 1, n - 1), 0, 0),
                         memory_space=pltpu.SMEM),
            pl.BlockSpec(memory_space=pl.ANY),
            pl.BlockSpec((tm, d), lambda i: (i, 0)),
            pl.BlockSpec((tm, lanes), lambda i: (i, 0)),
            pl.BlockSpec((1, 1, d), lambda i: (i // tps, 0, 0)),
            pl.BlockSpec((1, d), lambda i: (0, 0)),
            pl.BlockSpec((1, d), lambda i: (0, 0)),
        ],
        out_specs=pl.BlockSpec((tm, d), lambda i: (i, 0)),
        out_shape=jax.ShapeDtypeStruct((t, d), F32),
        scratch_shapes=[
            pltpu.VMEM((2, TOP_K * tm, d), F32),
            pltpu.SemaphoreType.DMA((2,)),
        ],
        compiler_params=pltpu.CompilerParams(
            dimension_semantics=("arbitrary",), vmem_limit_bytes=_vmem_limit(vmem)),
        name="combine",
    )(pos3, pos3, ys, x1, tw, g2, l2g, l2b)


def _routing_tables(counts, idx4, rank4, n_items):
    e = counts.shape[0]
    padded = (counts + MOE_TM - 1) // MOE_TM * MOE_TM
    pad_end = jnp.cumsum(padded)
    pad_start = pad_end - padded
    pos = pad_start[idx4] + rank4

    nsub_e = padded // MOE_TM
    items_e = (nsub_e + MOE_SUB - 1) // MOE_SUB
    item_end = jnp.cumsum(items_e)
    item_start = item_end - items_e
    total = item_end[-1]
    w = jnp.arange(n_items, dtype=I32)
    last = jnp.maximum(total - 1, 0)
    wv = jnp.minimum(w, last)
    ew = jnp.minimum(jnp.sum((item_end[None, :] <= wv[:, None]).astype(I32), axis=1), e - 1)
    s = wv - item_start[ew]
    valid = w < total
    item_row = jnp.where(valid, pad_start[ew] + s * (MOE_SUB * MOE_TM), 0).astype(I32)
    item_n = jnp.where(valid, jnp.clip(nsub_e[ew] - s * MOE_SUB, 0, MOE_SUB), 0).astype(I32)
    tail_tile = (pad_end[-1:] // MOE_TM).astype(I32)
    group_tbl = jnp.stack([counts, pad_start]).astype(I32)
    return pos.astype(I32), ew, item_row, item_n, tail_tile, group_tbl


def kernel(x, c, w_ada, b_ada, w_in, w_gk, b_gk, w_pool, b_pool, pool_scale, gla_norm_w, w_out,
           ln1_g, ln1_b, w_router, b_router, w_gate, b_gate, w_up, b_up, w_down, b_down,
           ln2_g, ln2_b):
    batch, seq, d = x.shape
    t = batch * seq
    e = w_router.shape[-1]
    lanes = V7X_LANES
    for l in range(w_ada.shape[0]):
        mod = _mod_call(c, w_ada[l], b_ada[l])
        sh1, sc1, g1, sh2, sc2, g2 = [m.reshape(batch, 1, d) for m in jnp.split(mod, 6, axis=-1)]

        wi = w_in[l]
        lr0 = POOL_WIDTH + 2 * GLA_K_WIDTH + GLA_V_WIDTH
        w_re = jnp.concatenate(
            [wi[:, :lr0], wi[:, lr0 + GLA_GATE_RANK:], wi[:, lr0:lr0 + GLA_GATE_RANK],
             jnp.zeros((d, lanes - GLA_GATE_RANK), wi.dtype)], axis=1).astype(BF16)
        x2 = x.reshape(t, d)
        proj = _inproj_call(x2, sh1, sc1, w_re, seq)

        wgk_pad = jnp.concatenate(
            [w_gk[l], jnp.zeros((lanes - GLA_GATE_RANK, GLA_K_WIDTH), w_gk.dtype)], axis=0).astype(BF16)
        y_gla = _gla_call(proj, wgk_pad, b_gk[l].reshape(1, -1), gla_norm_w[l].reshape(1, -1), batch, seq)

        wr = w_router[l]
        wr_hi = wr.astype(BF16)
        wr_lo = (wr - wr_hi.astype(F32)).astype(BF16)
        wra = jnp.concatenate([wr_hi, wr_lo, jnp.zeros((d, lanes - 2 * e), BF16)], axis=1)
        wrb = jnp.concatenate([wr_hi, jnp.zeros((d, lanes - e), BF16)], axis=1)
        br = jnp.concatenate([b_router[l], jnp.zeros((lanes - e,), F32)]).reshape(1, lanes)
        x1, h2, idx, tw, rank, cnt = _outproj_call(
            proj, y_gla, x2, g1, sh2, sc2,
            w_pool[l].astype(BF16), b_pool[l].reshape(1, -1), pool_scale[l].reshape(1, -1),
            w_out[l].astype(BF16), ln1_g[l].reshape(1, -1), ln1_b[l].reshape(1, -1),
            wra, wrb, br, seq)

        n_slots = t * TOP_K + e * MOE_TM
        n_items = e + n_slots // (MOE_SUB * MOE_TM)
        counts = cnt[0, :e].astype(I32)
        pos, item_e, item_row, item_n, tail_tile, group_tbl = _routing_tables(
            counts, idx[:, :TOP_K], rank[:, :TOP_K], n_items)

        xs = _dispatch_call(group_tbl, pos, h2, n_slots)
        ys = _moe_call(item_e, item_row, item_n, tail_tile, xs, w_gate[l], w_up[l], w_down[l],
                       b_gate[l], b_up[l], b_down[l])
        out = _combine_call(pos, ys, x1, tw, g2, ln2_g[l].reshape(1, -1), ln2_b[l].reshape(1, -1), seq)
        x = out.reshape(batch, seq, d)
    return x
```

```python
import functools

import jax
import jax.numpy as jnp
from jax import lax
from jax.experimental import pallas as pl
from jax.experimental.pallas import tpu as pltpu

F32 = jnp.float32
BF16 = jnp.bfloat16
I32 = jnp.int32

D_MODEL = 2048
POOL_WIDTH = 1024
POOL_GROUPS = 4
POOL_GROUP_DIM = 256
POOL_WINDOWS = (2, 4, 8, 16)
POOL_HALO = 16
GLA_HEADS = 4
GLA_HEAD_K = 128
GLA_HEAD_V = 256
GLA_K_WIDTH = GLA_HEADS * GLA_HEAD_K
GLA_V_WIDTH = GLA_HEADS * GLA_HEAD_V
GLA_GATE_RANK = 16
GLA_GATE_TAU = 16.0
GLA_CHUNK = 64
GLA_NORM_EPS = 1e-6
N_EXPERTS = 32
TOP_K = 4
D_FF = 2048
SWIGLU_ALPHA = 1.702
SWIGLU_LIMIT = 7.0
DEPTH = 1
DEEPNORM_ALPHA = (2.0 * DEPTH) ** 0.25
LN_EPS = 1e-5

V7X_LANES = 128
V7X_SUBLANES = 8
V7X_VMEM_BYTES = 64 * 1024 * 1024

MOD_TN = 512
MOD_KC = 64
INPROJ_TM = 512
INPROJ_TN = 512
PROJ_WIDTH = 4096 + V7X_LANES
GLA_BLOCK = 256
OUT_TM = 512
OUT_PARTS = 2
DISPATCH_TM = 256
MOE_TM = 256
MOE_SUB = 9
MOE_TF = 256
MOE_UNROLL = 4
COMBINE_TM = 128


def _vmem_limit(nbytes):
    return int(min(nbytes, V7X_VMEM_BYTES - 4 * 1024 * 1024))


def _layer_norm(x):
    mu = jnp.mean(x, axis=-1, keepdims=True)
    xc = x - mu
    var = jnp.mean(xc * xc, axis=-1, keepdims=True)
    return xc * lax.rsqrt(var + LN_EPS)


def _mod_kernel(c_ref, w_ref, b_ref, o_ref):
    nb = c_ref.shape[0]
    d, tn = w_ref.shape
    rep = tn // V7X_LANES
    g = MOD_KC // V7X_SUBLANES

    def body(i, accs):
        k0 = pl.multiple_of(i * MOD_KC, MOD_KC)
        w = w_ref[pl.ds(k0, MOD_KC), :].reshape(g, V7X_SUBLANES, tn)
        out = []
        for b in range(nb):
            cb = c_ref[b, pl.ds(k0, MOD_KC), :]
            s = cb * jax.nn.sigmoid(cb)
            s = jnp.concatenate([s] * rep, axis=1).reshape(g, V7X_SUBLANES, tn)
            out.append(accs[b] + jnp.sum(w * s, axis=0))
        return tuple(out)

    init = tuple(jnp.zeros((V7X_SUBLANES, tn), F32) for _ in range(nb))
    accs = lax.fori_loop(0, d // MOD_KC, body, init)
    rows = [jnp.sum(a, axis=0, keepdims=True) for a in accs]
    rows.append(jnp.zeros((V7X_SUBLANES - nb, tn), F32))
    o_ref[...] = jnp.concatenate(rows, axis=0) + b_ref[...]


def _mod_call(c, w_ada, b_ada):
    nb, d = c.shape
    n = w_ada.shape[1]
    c_rep = jnp.broadcast_to(c[:, :, None], (nb, d, V7X_LANES))
    out = pl.pallas_call(
        _mod_kernel,
        grid=(n // MOD_TN,),
        in_specs=[
            pl.BlockSpec((nb, d, V7X_LANES), lambda j: (0, 0, 0)),
            pl.BlockSpec((d, MOD_TN), lambda j: (0, j)),
            pl.BlockSpec((1, MOD_TN), lambda j: (0, j)),
        ],
        out_specs=pl.BlockSpec((V7X_SUBLANES, MOD_TN), lambda j: (0, j)),
        out_shape=jax.ShapeDtypeStruct((V7X_SUBLANES, n), F32),
        compiler_params=pltpu.CompilerParams(dimension_semantics=("arbitrary",)),
        name="mod",
    )(c_rep, w_ada, b_ada.reshape(1, n))
    return out[:nb]


def _inproj_kernel(x_ref, sh_ref, sc_ref, w_ref, o_ref):
    h = _layer_norm(x_ref[...]) * (1.0 + sc_ref[0]) + sh_ref[0]
    hb = h.astype(BF16)
    n = w_ref.shape[1]
    for n0 in range(0, n, INPROJ_TN):
        n1 = min(n0 + INPROJ_TN, n)
        o_ref[:, n0:n1] = jnp.dot(hb, w_ref[:, n0:n1], preferred_element_type=F32)


def _inproj_call(x2, sh1, sc1, w_re, seq):
    t, d = x2.shape
    n = w_re.shape[1]
    tpb = seq // INPROJ_TM
    vmem = 2 * INPROJ_TM * d * 4 + 2 * INPROJ_TM * n * 4 + d * n * 2 + 8 * 1024 * 1024
    return pl.pallas_call(
        _inproj_kernel,
        grid=(t // INPROJ_TM,),
        in_specs=[
            pl.BlockSpec((INPROJ_TM, d), lambda i: (i, 0)),
            pl.BlockSpec((1, 1, d), lambda i: (i // tpb, 0, 0)),
            pl.BlockSpec((1, 1, d), lambda i: (i // tpb, 0, 0)),
            pl.BlockSpec((d, n), lambda i: (0, 0), pipeline_mode=pl.Buffered(1)),
        ],
        out_specs=pl.BlockSpec((INPROJ_TM, n), lambda i: (i, 0)),
        out_shape=jax.ShapeDtypeStruct((t, n), F32),
        compiler_params=pltpu.CompilerParams(
            dimension_semantics=("arbitrary",), vmem_limit_bytes=_vmem_limit(vmem)),
        name="inproj",
    )(x2, sh1, sc1, w_re)


def _chunk_cumsum(x, row_in_chunk):
    s = 1
    while s < GLA_CHUNK:
        x = x + jnp.where(row_in_chunk >= s, pltpu.roll(x, s, 0), 0.0)
        s *= 2
    return x


def _group_ref(b, m, row):
    n, dk = b.shape
    g = 2 * m
    if g >= V7X_SUBLANES:
        r = b.reshape(n // g, g, dk)[:, m - 1:m, :]
        return jnp.broadcast_to(r, (n // g, g, dk)).reshape(n, dk)
    rr = row & (g - 1)
    out = None
    for r in range(g):
        sh = (r - (m - 1)) % n
        shifted = b if sh == 0 else pltpu.roll(b, sh, 0)
        out = shifted if out is None else jnp.where(rr == r, shifted, out)
    return out


def _dot_nt(a, b):
    return lax.dot_general(a, b, (((1,), (1,)), ((), ())), preferred_element_type=F32)


def _gla_kernel(q_ref, k_ref, v_ref, g_ref, lr_ref, wgk_ref, bgk_ref, nw_ref, o_ref, state_ref):
    n = q_ref.shape[0]
    dk, dv = GLA_HEAD_K, GLA_HEAD_V
    n_chunks = n // GLA_CHUNK

    @pl.when(pl.program_id(1) == 0)
    def _():
        state_ref[...] = jnp.zeros_like(state_ref)

    row = lax.broadcasted_iota(I32, (n, dk), 0)
    row_in_chunk = row & (GLA_CHUNK - 1)
    ti = lax.broadcasted_iota(I32, (n, n), 0)
    si = lax.broadcasted_iota(I32, (n, n), 1)
    x = ti ^ si
    level = jnp.zeros((n, n), I32)
    m = 1
    while m < GLA_CHUNK:
        level = level + (x >= m).astype(I32)
        m *= 2
    level = jnp.where((si <= ti) & (x < GLA_CHUNK), level, -1)

    z = jnp.dot(lr_ref[...].astype(BF16), wgk_ref[...], preferred_element_type=F32) + bgk_ref[...]
    log_alpha = (jnp.minimum(z, 0.0) - jnp.log(1.0 + jnp.exp(-jnp.abs(z)))) * (1.0 / GLA_GATE_TAU)

    scale = GLA_HEAD_K ** -0.5
    for h in range(GLA_HEADS):
        q = q_ref[:, h * dk:(h + 1) * dk] * scale
        k = k_ref[:, h * dk:(h + 1) * dk]
        vb = v_ref[:, h * dv:(h + 1) * dv].astype(BF16)
        bc = _chunk_cumsum(log_alpha[:, h * dk:(h + 1) * dk], row_in_chunk)

        scores = jnp.where(level == 0, _dot_nt(q.astype(BF16), k.astype(BF16)), 0.0)
        m = 1
        lvl = 1
        while m < GLA_CHUNK:
            e = jnp.exp(-jnp.abs(bc - _group_ref(bc, m, row)))
            s = _dot_nt((q * e).astype(BF16), (k * e).astype(BF16))
            scores = jnp.where(level == lvl, s, scores)
            m *= 2
            lvl += 1
        o = jnp.dot(scores.astype(BF16), vb, preferred_element_type=F32)

        qd = (q * jnp.exp(bc)).astype(BF16)
        inter = []
        for c in range(n_chunks):
            r0 = c * GLA_CHUNK
            r1 = r0 + GLA_CHUNK
            state = state_ref[h]
            inter.append(jnp.dot(qd[r0:r1], state.astype(BF16), preferred_element_type=F32))
            bcc = bc[r0:r1]
            blast = bcc[GLA_CHUNK - 1:GLA_CHUNK, :]
            kd_t = (k[r0:r1] * jnp.exp(blast - bcc)).T.astype(BF16)
            decay = jnp.exp(bcc.T[:, GLA_CHUNK - 1:GLA_CHUNK])
            state_ref[h] = decay * state + jnp.dot(kd_t, vb[r0:r1], preferred_element_type=F32)
        o = o + jnp.concatenate(inter, axis=0)

        o = o * lax.rsqrt(jnp.mean(o * o, axis=-1, keepdims=True) + GLA_NORM_EPS) * nw_ref[...]
        g = g_ref[:, h * dv:(h + 1) * dv]
        o_ref[:, h * dv:(h + 1) * dv] = (o * (g * jax.nn.sigmoid(g))).astype(o_ref.dtype)


def _gla_call(proj, wgk_pad, bgk, norm_w, batch, seq):
    t = proj.shape[0]
    nb = seq // GLA_BLOCK
    kw, vw = GLA_K_WIDTH, GLA_V_WIDTH
    q_col = POOL_WIDTH // kw
    v_col = (POOL_WIDTH + 2 * kw) // vw
    g_col = v_col + 1
    lr_col = (POOL_WIDTH + 2 * kw + 2 * vw) // V7X_LANES
    row = lambda b, j: b * nb + j
    return pl.pallas_call(
        _gla_kernel,
        grid=(batch, nb),
        in_specs=[
            pl.BlockSpec((GLA_BLOCK, kw), lambda b, j: (row(b, j), q_col)),
            pl.BlockSpec((GLA_BLOCK, kw), lambda b, j: (row(b, j), q_col + 1)),
            pl.BlockSpec((GLA_BLOCK, vw), lambda b, j: (row(b, j), v_col)),
            pl.BlockSpec((GLA_BLOCK, vw), lambda b, j: (row(b, j), g_col)),
            pl.BlockSpec((GLA_BLOCK, V7X_LANES), lambda b, j: (row(b, j), lr_col)),
            pl.BlockSpec((V7X_LANES, kw), lambda b, j: (0, 0)),
            pl.BlockSpec((1, kw), lambda b, j: (0, 0)),
            pl.BlockSpec((1, GLA_HEAD_V), lambda b, j: (0, 0)),
        ],
        out_specs=pl.BlockSpec((GLA_BLOCK, vw), lambda b, j: (row(b, j), 0)),
        out_shape=jax.ShapeDtypeStruct((t, vw), BF16),
        scratch_shapes=[pltpu.VMEM((GLA_HEADS, GLA_HEAD_K, GLA_HEAD_V), F32)],
        compiler_params=pltpu.CompilerParams(dimension_semantics=("arbitrary", "arbitrary")),
        name="gla",
    )(proj, proj, proj, proj, proj, wgk_pad, bgk, norm_w)


def _outproj_kernel(u_ref, yg_ref, x_ref, g1_ref, sh2_ref, sc2_ref, wp_ref, bp_ref, ps_ref, wo_ref,
                    l1g_ref, l1b_ref, wra_ref, wrb_ref, br_ref,
                    x1_ref, h2_ref, idx_ref, tw_ref, rank_ref, cnt_ref,
                    halo_ref, carry_ref, *, tiles_per_seq):
    i = pl.program_id(0)
    tm = u_ref.shape[0]
    seq_tile = i % tiles_per_seq

    @pl.when(seq_tile == 0)
    def _():
        halo_ref[...] = jnp.zeros_like(halo_ref)

    @pl.when(i == 0)
    def _():
        carry_ref[...] = jnp.zeros_like(carry_ref)

    gd = POOL_GROUP_DIM
    part_rows = tm // OUT_PARTS
    lane = lax.broadcasted_iota(I32, (part_rows, V7X_LANES), 1)
    neg_inf = jnp.float32(-jnp.inf)
    tri = (lax.broadcasted_iota(I32, (part_rows, part_rows), 1)
           < lax.broadcasted_iota(I32, (part_rows, part_rows), 0)).astype(BF16)

    def mix(r0, halo):
        rows = slice(r0, r0 + part_rows)
        u = u_ref[rows, :]
        pos1 = seq_tile * tm + r0 + lax.broadcasted_iota(I32, (part_rows, gd), 0) + 1
        mixed = []
        for gi, win in enumerate(POOL_WINDOWS):
            cols = slice(gi * gd, (gi + 1) * gd)
            ug = u[:, cols]
            ws = jnp.concatenate([halo[:, cols], ug], axis=0)
            s = 1
            while s < win:
                ws = ws + pltpu.roll(ws, s, 0)
                s *= 2
            count = jnp.minimum(pos1, win).astype(F32)
            pooled = ws[POOL_HALO:, :] / count - ug
            m = jnp.dot(pooled.astype(BF16), wp_ref[gi], preferred_element_type=F32)
            mixed.append(((m + bp_ref[:, cols]) * ps_ref[:, cols]).astype(BF16))
        cat = jnp.concatenate(mixed + [yg_ref[rows, :]], axis=1)
        return jnp.dot(cat, wo_ref[...], preferred_element_type=F32)

    def norms(r0, y):
        rows = slice(r0, r0 + part_rows)
        x1 = _layer_norm(DEEPNORM_ALPHA * x_ref[rows, :] + g1_ref[0] * y) * l1g_ref[...] + l1b_ref[...]
        x1_ref[rows, :] = x1
        h2 = _layer_norm(x1) * (1.0 + sc2_ref[0]) + sh2_ref[0]
        h2_ref[rows, :] = h2
        hb = h2.astype(BF16)
        return hb, (h2 - hb.astype(F32)).astype(BF16)

    def route(h_hi, h_lo):
        ra = jnp.dot(h_hi, wra_ref[...], preferred_element_type=F32)
        rb = jnp.dot(h_lo, wrb_ref[...], preferred_element_type=F32)
        return ra + pltpu.roll(ra, V7X_LANES - N_EXPERTS, 1) + rb + br_ref[...]

    def select(r0, logits, carry):
        rows = slice(r0, r0 + part_rows)
        work = jnp.where(lane < N_EXPERTS, logits, neg_inf)
        vals, idxs = [], []
        for _ in range(TOP_K):
            mx = jnp.max(work, axis=-1, keepdims=True)
            sel = jnp.min(jnp.where(work == mx, lane, V7X_LANES), axis=-1, keepdims=True)
            vals.append(mx)
            idxs.append(sel)
            work = jnp.where(lane == sel, neg_inf, work)
        exps = [jnp.exp(v - vals[0]) for v in vals]
        denom = exps[0] + exps[1] + exps[2] + exps[3]

        chosen = jnp.where((lane < N_EXPERTS) & (work == neg_inf), 1.0, 0.0)
        excl = jnp.dot(tri, chosen.astype(BF16), preferred_element_type=F32) + carry
        idx_out = jnp.zeros((part_rows, V7X_LANES), I32)
        tw_out = jnp.zeros((part_rows, V7X_LANES), F32)
        rank_out = jnp.zeros((part_rows, V7X_LANES), F32)
        for kk in range(TOP_K):
            rk = jnp.sum(jnp.where(lane == idxs[kk], excl, 0.0), axis=-1, keepdims=True)
            idx_out = jnp.where(lane == kk, idxs[kk], idx_out)
            tw_out = jnp.where(lane == kk, exps[kk] / denom, tw_out)
            rank_out = jnp.where(lane == kk, rk, rank_out)
        idx_ref[rows, :] = idx_out
        tw_ref[rows, :] = tw_out
        rank_ref[rows, :] = rank_out.astype(I32)
        return carry + jnp.sum(chosen, axis=0, keepdims=True)

    starts = [p * part_rows for p in range(OUT_PARTS)]
    halos = [halo_ref[...]] + [u_ref[r0 - POOL_HALO:r0, :] for r0 in starts[1:]]
    ys = [mix(r0, halo) for r0, halo in zip(starts, halos)]
    hs = [norms(r0, y) for r0, y in zip(starts, ys)]
    logits = [route(h_hi, h_lo) for h_hi, h_lo in hs]
    carry = carry_ref[0:1, :]
    for r0, lg in zip(starts, logits):
        carry = select(r0, lg, carry)
    halo_ref[...] = u_ref[tm - POOL_HALO:tm, :]
    carry_ref[...] = jnp.broadcast_to(carry, carry_ref.shape)
    cnt_ref[...] = jnp.broadcast_to(carry, cnt_ref.shape)


def _outproj_call(proj, y_gla, x2, g1, sh2, sc2, wp, bp, ps, wo, l1g, l1b, wra, wrb, br, seq):
    t, d = x2.shape
    tm = OUT_TM
    tps = seq // tm
    bidx = lambda i: (i // tps, 0, 0)
    const2 = lambda i: (0, 0)
    row = lambda i: (i, 0)
    lanes = V7X_LANES
    vmem = (2 * tm * (POOL_WIDTH * 4 + GLA_V_WIDTH * 2 + d * 4 + d * 4 + d * 4 + 3 * lanes * 4)
            + d * d * 2 + 16 * 1024 * 1024)
    return pl.pallas_call(
        functools.partial(_outproj_kernel, tiles_per_seq=tps),
        grid=(t // tm,),
        in_specs=[
            pl.BlockSpec((tm, POOL_WIDTH), row),
            pl.BlockSpec((tm, GLA_V_WIDTH), row),
            pl.BlockSpec((tm, d), row),
            pl.BlockSpec((1, 1, d), bidx),
            pl.BlockSpec((1, 1, d), bidx),
            pl.BlockSpec((1, 1, d), bidx),
            pl.BlockSpec((POOL_GROUPS, POOL_GROUP_DIM, POOL_GROUP_DIM), lambda i: (0, 0, 0)),
            pl.BlockSpec((1, POOL_WIDTH), const2),
            pl.BlockSpec((1, POOL_WIDTH), const2),
            pl.BlockSpec((d, d), const2, pipeline_mode=pl.Buffered(1)),
            pl.BlockSpec((1, d), const2),
            pl.BlockSpec((1, d), const2),
            pl.BlockSpec((d, lanes), const2),
            pl.BlockSpec((d, lanes), const2),
            pl.BlockSpec((1, lanes), const2),
        ],
        out_specs=[
            pl.BlockSpec((tm, d), row),
            pl.BlockSpec((tm, d), row),
            pl.BlockSpec((tm, lanes), row),
            pl.BlockSpec((tm, lanes), row),
            pl.BlockSpec((tm, lanes), row),
            pl.BlockSpec((V7X_SUBLANES, lanes), const2),
        ],
        out_shape=[
            jax.ShapeDtypeStruct((t, d), F32),
            jax.ShapeDtypeStruct((t, d), F32),
            jax.ShapeDtypeStruct((t, lanes), I32),
            jax.ShapeDtypeStruct((t, lanes), F32),
            jax.ShapeDtypeStruct((t, lanes), I32),
            jax.ShapeDtypeStruct((V7X_SUBLANES, lanes), F32),
        ],
        scratch_shapes=[
            pltpu.VMEM((POOL_HALO, POOL_WIDTH), F32),
            pltpu.VMEM((V7X_SUBLANES, lanes), F32),
        ],
        compiler_params=pltpu.CompilerParams(
            dimension_semantics=("arbitrary",), vmem_limit_bytes=_vmem_limit(vmem)),
        name="outproj",
    )(proj, y_gla, x2, g1, sh2, sc2, wp, bp, ps, wo, l1g, l1b, wra, wrb, br)


def _dispatch_kernel(tbl_ref, pos_ref, h_ref, xs_ref, zero_ref, sem, zsem):
    tm = h_ref.shape[0]
    n_experts = tbl_ref.shape[1]
    n_tiles = xs_ref.shape[0] // MOE_TM

    def pad_copy(p):
        return pltpu.make_async_copy(zero_ref.at[pl.ds(0, 1)], xs_ref.at[pl.ds(p, 1)], zsem)

    def tail_copy(tile):
        dst = xs_ref.at[pl.ds(pl.multiple_of(tile * MOE_TM, MOE_TM), MOE_TM)]
        return pltpu.make_async_copy(zero_ref, dst, zsem)

    def for_each_pad(fn):
        def per_expert(e, c):
            cnt = tbl_ref[0, e]
            first = tbl_ref[1, e] + cnt
            n_pad = (cnt + MOE_TM - 1) // MOE_TM * MOE_TM - cnt

            def per_row(r, c2):
                fn(pad_copy(first + r))
                return c2
            lax.fori_loop(0, n_pad, per_row, 0)
            return c
        lax.fori_loop(0, n_experts, per_expert, 0)
        last = n_experts - 1
        tail0 = (tbl_ref[1, last] + tbl_ref[0, last] + MOE_TM - 1) // MOE_TM

        def per_tile(r, c):
            fn(tail_copy(tail0 + r))
            return c
        lax.fori_loop(0, n_tiles - tail0, per_tile, 0)

    @pl.when(pl.program_id(0) == 0)
    def _():
        zero_ref[...] = jnp.zeros_like(zero_ref)
        for_each_pad(lambda cp: cp.start())
        for_each_pad(lambda cp: cp.wait())

    def row_copy(t, p):
        return pltpu.make_async_copy(h_ref.at[pl.ds(t, 1)], xs_ref.at[pl.ds(p, 1)], sem)

    def issue(t, carry):
        for kk in range(TOP_K):
            row_copy(t, pos_ref[0, 0, t * TOP_K + kk]).start(priority=kk % 2)
        return carry

    lax.fori_loop(0, tm, issue, 0, unroll=2)
    for kk in range(TOP_K):
        pltpu.make_async_copy(h_ref, xs_ref.at[pl.ds(0, tm)], sem).wait()


def _dispatch_call(tbl, pos, h2, n_slots):
    t, w = h2.shape
    tm = DISPATCH_TM
    pos3 = pos.reshape(t // tm, 1, tm * TOP_K)
    return pl.pallas_call(
        _dispatch_kernel,
        grid=(t // tm,),
        in_specs=[
            pl.BlockSpec(memory_space=pltpu.SMEM),
            pl.BlockSpec((1, 1, tm * TOP_K), lambda i: (i, 0, 0), memory_space=pltpu.SMEM),
            pl.BlockSpec((tm, w), lambda i: (i, 0)),
        ],
        out_specs=pl.BlockSpec(memory_space=pl.ANY),
        out_shape=jax.ShapeDtypeStruct((n_slots, w), h2.dtype),
        scratch_shapes=[
            pltpu.VMEM((MOE_TM, w), h2.dtype),
            pltpu.SemaphoreType.DMA(()),
            pltpu.SemaphoreType.DMA(()),
        ],
        compiler_params=pltpu.CompilerParams(dimension_semantics=("arbitrary",)),
        name="dispatch",
    )(tbl, pos3, h2)


def _moe_kernel(item_e_ref, item_row_ref, item_n_ref, tail_ref,
                xs_ref, wg_ref, wu_ref, wd_ref, bg_ref, bu_ref, bd_ref,
                ys_ref,
                xbuf, stage, yacc, wgu_bf, wd_bf, sem_in, sem_out):
    del item_e_ref
    i = pl.program_id(0)
    j = pl.program_id(1)
    n_items = pl.num_programs(0)
    n_chunks = pl.num_programs(1)
    nsub = item_n_ref[i]
    row0 = item_row_ref[i]
    tm, tf = MOE_TM, MOE_TF

    def in_copy(r):
        row = jnp.minimum(row0 + r * tm, xs_ref.shape[0] - tm)
        src = xs_ref.at[pl.ds(pl.multiple_of(row, tm), tm)]
        return pltpu.make_async_copy(src, stage.at[r & 1], sem_in.at[r & 1])

    def out_copy(r):
        dst = ys_ref.at[pl.ds(pl.multiple_of(row0 + r * tm, tm), tm)]
        return pltpu.make_async_copy(yacc.at[pl.ds(pl.multiple_of(r * tm, tm), tm)], dst, sem_out.at[r])

    @pl.when((j == 0) & (nsub > 0))
    def _():
        in_copy(0).start()
        in_copy(1).start()

    def chunk(first, last):
        wgu_bf[:, :tf] = wg_ref[0].astype(BF16)
        wgu_bf[:, tf:] = wu_ref[0].astype(BF16)
        wd_bf[...] = wd_ref[0].astype(BF16)

        def activation(r):
            rows = pl.ds(pl.multiple_of(r * tm, tm), tm)
            if first:
                in_copy(r).wait()
                xb = stage[r & 1].astype(BF16)
                xbuf[rows, :] = xb
                in_copy(r + 2).start()
            else:
                xb = xbuf[rows, :]
            gu = jnp.dot(xb, wgu_bf[...], preferred_element_type=F32)
            gate = jnp.minimum(gu[:, :tf] + bg_ref[0], SWIGLU_LIMIT)
            up = jnp.clip(gu[:, tf:] + bu_ref[0], -SWIGLU_LIMIT, SWIGLU_LIMIT)
            return (gate * jax.nn.sigmoid(SWIGLU_ALPHA * gate) * (up + 1.0)).astype(BF16)

        def down(r, act):
            rows = pl.ds(pl.multiple_of(r * tm, tm), tm)
            part = jnp.dot(act, wd_bf[...], preferred_element_type=F32)
            if first:
                yacc[rows, :] = part + bd_ref[0]
            else:
                yacc[rows, :] = yacc[rows, :] + part
            if last:
                out_copy(r).start()

        def step(r, act):
            nxt = activation(r + 1)
            down(r, act)
            return nxt

        def steps_from(r, count, act):
            for c in range(count):
                act = step(r + c, act)
            return act

        n_steps = nsub - 1
        n_full = n_steps // MOE_UNROLL
        act = lax.fori_loop(0, n_full, lambda p, a: steps_from(MOE_UNROLL * p, MOE_UNROLL, a),
                            activation(0))
        done = n_full * MOE_UNROLL
        rem = n_steps - done
        act = lax.cond(rem >= 2, lambda a: steps_from(done, 2, a), lambda a: a, act)
        done = done + jnp.where(rem >= 2, 2, 0)
        act = lax.cond(rem % 2 == 1, lambda a: step(done, a), lambda a: a, act)
        down(nsub - 1, act)

        if first:
            in_copy(0).wait()
            in_copy(1).wait()
        if last:
            def drain(r, c):
                out_copy(r).wait()
                return c
            lax.fori_loop(0, nsub, drain, 0)

    @pl.when((nsub > 0) & (j == 0))
    def _():
        chunk(True, False)

    @pl.when((nsub > 0) & (j > 0) & (j < n_chunks - 1))
    def _():
        chunk(False, False)

    @pl.when((nsub > 0) & (j == n_chunks - 1))
    def _():
        chunk(False, True)

    @pl.when((i == n_items - 1) & (j == n_chunks - 1))
    def _():
        yacc[0:tm, :] = jnp.zeros((tm, D_MODEL), F32)
        tail0 = tail_ref[0]
        n_tail = ys_ref.shape[0] // tm - tail0

        def tail_copy(r):
            dst = ys_ref.at[pl.ds(pl.multiple_of((tail0 + r) * tm, tm), tm)]
            return pltpu.make_async_copy(yacc.at[0:tm], dst, sem_out.at[0])

        def start(r, c):
            tail_copy(r).start()
            return c

        def drain(r, c):
            tail_copy(r).wait()
            return c

        lax.fori_loop(0, n_tail, start, 0)
        lax.fori_loop(0, n_tail, drain, 0)


def _moe_call(item_e, item_row, item_n, tail_tile, xs, wg, wu, wd, bg, bu, bd):
    n_slots, d = xs.shape
    e, _, f = wg.shape
    n_items = item_e.shape[0]
    n_chunks = f // MOE_TF
    rows = MOE_SUB * MOE_TM

    def jeff(i, j, n_ref):
        return jnp.where(n_ref[i] > 0, j, n_chunks - 1)

    vmem = (rows * d * 2 + 2 * MOE_TM * d * 4 + rows * d * 4 + 3 * 2 * d * MOE_TF * 4
            + 3 * d * MOE_TF * 2 + 6 * 1024 * 1024)
    grid_spec = pltpu.PrefetchScalarGridSpec(
        num_scalar_prefetch=4,
        grid=(n_items, n_chunks),
        in_specs=[
            pl.BlockSpec(memory_space=pl.ANY),
            pl.BlockSpec((1, d, MOE_TF), lambda i, j, e_r, r_r, n_r, t_r: (e_r[i], 0, jeff(i, j, n_r))),
            pl.BlockSpec((1, d, MOE_TF), lambda i, j, e_r, r_r, n_r, t_r: (e_r[i], 0, jeff(i, j, n_r))),
            pl.BlockSpec((1, MOE_TF, d), lambda i, j, e_r, r_r, n_r, t_r: (e_r[i], jeff(i, j, n_r), 0)),
            pl.BlockSpec((1, 1, MOE_TF), lambda i, j, e_r, r_r, n_r, t_r: (e_r[i], 0, jeff(i, j, n_r))),
            pl.BlockSpec((1, 1, MOE_TF), lambda i, j, e_r, r_r, n_r, t_r: (e_r[i], 0, jeff(i, j, n_r))),
            pl.BlockSpec((1, 1, d), lambda i, j, e_r, r_r, n_r, t_r: (e_r[i], 0, 0)),
        ],
        out_specs=pl.BlockSpec(memory_space=pl.ANY),
        scratch_shapes=[
            pltpu.VMEM((rows, d), BF16),
            pltpu.VMEM((2, MOE_TM, d), F32),
            pltpu.VMEM((rows, d), F32),
            pltpu.VMEM((d, 2 * MOE_TF), BF16),
            pltpu.VMEM((MOE_TF, d), BF16),
            pltpu.SemaphoreType.DMA((2,)),
            pltpu.SemaphoreType.DMA((MOE_SUB,)),
        ],
    )
    return pl.pallas_call(
        _moe_kernel,
        grid_spec=grid_spec,
        out_shape=jax.ShapeDtypeStruct((n_slots, d), F32),
        compiler_params=pltpu.CompilerParams(
            dimension_semantics=("arbitrary", "arbitrary"), vmem_limit_bytes=_vmem_limit(vmem)),
        name="moe",
    )(item_e, item_row, item_n, tail_tile, xs, wg, wu, wd, bg.reshape(e, 1, f), bu.reshape(e, 1, f),
      bd.reshape(e, 1, d))


def _combine_kernel(pos_ref, posn_ref, ys_ref, x1_ref, tw_ref, g2_ref, l2g_ref, l2b_ref,
                    o_ref, buf, sem):
    i = pl.program_id(0)
    n = pl.num_programs(0)
    tm = x1_ref.shape[0]
    slot = i % 2

    def row_copy(p_ref, t, kk, s):
        p = p_ref[0, 0, t * TOP_K + kk]
        return pltpu.make_async_copy(ys_ref.at[pl.ds(p, 1)], buf.at[s, pl.ds(kk * tm + t, 1)], sem.at[s])

    def issue(p_ref, s):
        def body(t, c):
            for kk in range(TOP_K):
                row_copy(p_ref, t, kk, s).start(priority=kk % 2)
            return c
        lax.fori_loop(0, tm, body, 0, unroll=2)

    @pl.when(i == 0)
    def _():
        issue(pos_ref, 0)

    @pl.when(i + 1 < n)
    def _():
        issue(posn_ref, 1 - slot)

    pltpu.make_async_copy(ys_ref.at[pl.ds(0, TOP_K * tm)], buf.at[slot], sem.at[slot]).wait()

    tw = tw_ref[...]
    y = None
    for kk in range(TOP_K):
        part = tw[:, kk:kk + 1] * buf[slot, kk * tm:(kk + 1) * tm, :]
        y = part if y is None else y + part
    r = DEEPNORM_ALPHA * x1_ref[...] + g2_ref[0] * y
    o_ref[...] = _layer_norm(r) * l2g_ref[...] + l2b_ref[...]


def _combine_call(pos, ys, x1, tw, g2, l2g, l2b, seq):
    t, d = x1.shape
    tm = COMBINE_TM
    n = t // tm
    tps = seq // tm
    pos3 = pos.reshape(n, 1, tm * TOP_K)
    lanes = V7X_LANES
    vmem = 2 * TOP_K * tm * d * 4 + 4 * tm * d * 4 + 8 * 1024 * 1024
    return pl.pallas_call(
        _combine_kernel,
        grid=(n,),
        in_specs=[
            pl.BlockSpec((1, 1, tm * TOP_K), lambda i: (i, 0, 0), memory_space=pltpu.SMEM),
            pl.BlockSpec((1, 1, tm * TOP_K), lambda i: (jnp.minimum(i + 1, n - 1), 0, 0),
                         memory_space=pltpu.SMEM),
            pl.BlockSpec(memory_space=pl.ANY),
            pl.BlockSpec((tm, d), lambda i: (i, 0)),
            pl.BlockSpec((tm, lanes), lambda i: (i, 0)),
            pl.BlockSpec((1, 1, d), lambda i: (i // tps, 0, 0)),
            pl.BlockSpec((1, d), lambda i: (0, 0)),
            pl.BlockSpec((1, d), lambda i: (0, 0)),
        ],
        out_specs=pl.BlockSpec((tm, d), lambda i: (i, 0)),
        out_shape=jax.ShapeDtypeStruct((t, d), F32),
        scratch_shapes=[
            pltpu.VMEM((2, TOP_K * tm, d), F32),
            pltpu.SemaphoreType.DMA((2,)),
        ],
        compiler_params=pltpu.CompilerParams(
            dimension_semantics=("arbitrary",), vmem_limit_bytes=_vmem_limit(vmem)),
        name="combine",
    )(pos3, pos3, ys, x1, tw, g2, l2g, l2b)


def _routing_tables(counts, idx4, rank4, n_items):
    e = counts.shape[0]
    padded = (counts + MOE_TM - 1) // MOE_TM * MOE_TM
    pad_end = jnp.cumsum(padded)
    pad_start = pad_end - padded
    pos = pad_start[idx4] + rank4

    nsub_e = padded // MOE_TM
    items_e = (nsub_e + MOE_SUB - 1) // MOE_SUB
    item_end = jnp.cumsum(items_e)
    item_start = item_end - items_e
    total = item_end[-1]
    w = jnp.arange(n_items, dtype=I32)
    last = jnp.maximum(total - 1, 0)
    wv = jnp.minimum(w, last)
    ew = jnp.minimum(jnp.sum((item_end[None, :] <= wv[:, None]).astype(I32), axis=1), e - 1)
    s = wv - item_start[ew]
    valid = w < total
    item_row = jnp.where(valid, pad_start[ew] + s * (MOE_SUB * MOE_TM), 0).astype(I32)
    item_n = jnp.where(valid, jnp.clip(nsub_e[ew] - s * MOE_SUB, 0, MOE_SUB), 0).astype(I32)
    tail_tile = (pad_end[-1:] // MOE_TM).astype(I32)
    group_tbl = jnp.stack([counts, pad_start]).astype(I32)
    return pos.astype(I32), ew, item_row, item_n, tail_tile, group_tbl


def kernel(x, c, w_ada, b_ada, w_in, w_gk, b_gk, w_pool, b_pool, pool_scale, gla_norm_w, w_out,
           ln1_g, ln1_b, w_router, b_router, w_gate, b_gate, w_up, b_up, w_down, b_down,
           ln2_g, ln2_b):
    batch, seq, d = x.shape
    t = batch * seq
    e = w_router.shape[-1]
    lanes = V7X_LANES
    for l in range(w_ada.shape[0]):
        mod = _mod_call(c, w_ada[l], b_ada[l])
        sh1, sc1, g1, sh2, sc2, g2 = [m.reshape(batch, 1, d) for m in jnp.split(mod, 6, axis=-1)]

        wi = w_in[l]
        lr0 = POOL_WIDTH + 2 * GLA_K_WIDTH + GLA_V_WIDTH
        w_re = jnp.concatenate(
            [wi[:, :lr0], wi[:, lr0 + GLA_GATE_RANK:], wi[:, lr0:lr0 + GLA_GATE_RANK],
             jnp.zeros((d, lanes - GLA_GATE_RANK), wi.dtype)], axis=1).astype(BF16)
        x2 = x.reshape(t, d)
        proj = _inproj_call(x2, sh1, sc1, w_re, seq)

        wgk_pad = jnp.concatenate(
            [w_gk[l], jnp.zeros((lanes - GLA_GATE_RANK, GLA_K_WIDTH), w_gk.dtype)], axis=0).astype(BF16)
        y_gla = _gla_call(proj, wgk_pad, b_gk[l].reshape(1, -1), gla_norm_w[l].reshape(1, -1), batch, seq)

        wr = w_router[l]
        wr_hi = wr.astype(BF16)
        wr_lo = (wr - wr_hi.astype(F32)).astype(BF16)
        wra = jnp.concatenate([wr_hi, wr_lo, jnp.zeros((d, lanes - 2 * e), BF16)], axis=1)
        wrb = jnp.concatenate([wr_hi, jnp.zeros((d, lanes - e), BF16)], axis=1)
        br = jnp.concatenate([b_router[l], jnp.zeros((lanes - e,), F32)]).reshape(1, lanes)
        x1, h2, idx, tw, rank, cnt = _outproj_call(
            proj, y_gla, x2, g1, sh2, sc2,
            w_pool[l].astype(BF16), b_pool[l].reshape(1, -1), pool_scale[l].reshape(1, -1),
            w_out[l].astype(BF16), ln1_g[l].reshape(1, -1), ln1_b[l].reshape(1, -1),
            wra, wrb, br, seq)

        n_slots = t * TOP_K + e * MOE_TM
        n_items = e + n_slots // (MOE_SUB * MOE_TM)
        counts = cnt[0, :e].astype(I32)
        pos, item_e, item_row, item_n, tail_tile, group_tbl = _routing_tables(
            counts, idx[:, :TOP_K], rank[:, :TOP_K], n_items)

        xs = _dispatch_call(group_tbl, pos, h2, n_slots)
        ys = _moe_call(item_e, item_row, item_n, tail_tile, xs, w_gate[l], w_up[l], w_down[l],
                       b_gate[l], b_up[l], b_down[l])
        out = _combine_call(pos, ys, x1, tw, g2, ln2_g[l].reshape(1, -1), ln2_b[l].reshape(1, -1), seq)
        x = out.reshape(batch, seq, d)
    return x
```

```python
import functools

import jax
import jax.numpy as jnp
from jax import lax
from jax.experimental import pallas as pl
from jax.experimental.pallas import tpu as pltpu

F32 = jnp.float32
BF16 = jnp.bfloat16
I32 = jnp.int32

D_MODEL = 2048
POOL_WIDTH = 1024
POOL_GROUPS = 4
POOL_GROUP_DIM = 256
POOL_WINDOWS = (2, 4, 8, 16)
POOL_HALO = 16
GLA_HEADS = 4
GLA_HEAD_K = 128
GLA_HEAD_V = 256
GLA_K_WIDTH = GLA_HEADS * GLA_HEAD_K
GLA_V_WIDTH = GLA_HEADS * GLA_HEAD_V
GLA_GATE_RANK = 16
GLA_GATE_TAU = 16.0
GLA_CHUNK = 64
GLA_NORM_EPS = 1e-6
N_EXPERTS = 32
TOP_K = 4
D_FF = 2048
SWIGLU_ALPHA = 1.702
SWIGLU_LIMIT = 7.0
DEPTH = 1
DEEPNORM_ALPHA = (2.0 * DEPTH) ** 0.25
LN_EPS = 1e-5

V7X_LANES = 128
V7X_SUBLANES = 8
V7X_VMEM_BYTES = 64 * 1024 * 1024

MOD_TN = 512
MOD_KC = 64
INPROJ_TM = 512
INPROJ_TN = 512
PROJ_WIDTH = 4096 + V7X_LANES
GLA_BLOCK = 256
OUT_TM = 512
OUT_PARTS = 2
DISPATCH_TM = 256
MOE_TM = 256
MOE_SUB = 9
MOE_TF = 256
MOE_UNROLL = 4
COMBINE_TM = 128


def _vmem_limit(nbytes):
    return int(min(nbytes, V7X_VMEM_BYTES - 4 * 1024 * 1024))


def _layer_norm(x):
    mu = jnp.mean(x, axis=-1, keepdims=True)
    xc = x - mu
    var = jnp.mean(xc * xc, axis=-1, keepdims=True)
    return xc * lax.rsqrt(var + LN_EPS)


def _mod_kernel(c_ref, w_ref, b_ref, o_ref, s_ref):
    nb = c_ref.shape[0]
    d, tn = w_ref.shape
    rep = tn // V7X_LANES
    g = MOD_KC // V7X_SUBLANES

    @pl.when(pl.program_id(0) == 0)
    def _():
        for b in range(nb):
            cb = c_ref[b]
            s_ref[b] = cb * jax.nn.sigmoid(cb)

    def body(i, accs):
        k0 = pl.multiple_of(i * MOD_KC, MOD_KC)
        w = w_ref[pl.ds(k0, MOD_KC), :].reshape(g, V7X_SUBLANES, tn)
        out = []
        for b in range(nb):
            s = s_ref[b, pl.ds(k0, MOD_KC), :]
            s = jnp.concatenate([s] * rep, axis=1).reshape(g, V7X_SUBLANES, tn)
            out.append(accs[b] + jnp.sum(w * s, axis=0))
        return tuple(out)

    init = tuple(jnp.zeros((V7X_SUBLANES, tn), F32) for _ in range(nb))
    accs = lax.fori_loop(0, d // MOD_KC, body, init)
    rows = [jnp.sum(a, axis=0, keepdims=True) for a in accs]
    rows.append(jnp.zeros((V7X_SUBLANES - nb, tn), F32))
    o_ref[...] = jnp.concatenate(rows, axis=0) + b_ref[...]


def _mod_call(c, w_ada, b_ada):
    nb, d = c.shape
    n = w_ada.shape[1]
    c_rep = jnp.broadcast_to(c[:, :, None], (nb, d, V7X_LANES))
    out = pl.pallas_call(
        _mod_kernel,
        grid=(n // MOD_TN,),
        in_specs=[
            pl.BlockSpec((nb, d, V7X_LANES), lambda j: (0, 0, 0)),
            pl.BlockSpec((d, MOD_TN), lambda j: (0, j)),
            pl.BlockSpec((1, MOD_TN), lambda j: (0, j)),
        ],
        out_specs=pl.BlockSpec((V7X_SUBLANES, MOD_TN), lambda j: (0, j)),
        out_shape=jax.ShapeDtypeStruct((V7X_SUBLANES, n), F32),
        scratch_shapes=[pltpu.VMEM((nb, d, V7X_LANES), F32)],
        compiler_params=pltpu.CompilerParams(dimension_semantics=("arbitrary",)),
        name="mod",
    )(c_rep, w_ada, b_ada.reshape(1, n))
    return out[:nb]


def _inproj_kernel(x_ref, sh_ref, sc_ref, w_ref, o_ref):
    h = _layer_norm(x_ref[...]) * (1.0 + sc_ref[0]) + sh_ref[0]
    hb = h.astype(BF16)
    n = w_ref.shape[1]
    for n0 in range(0, n, INPROJ_TN):
        n1 = min(n0 + INPROJ_TN, n)
        o_ref[:, n0:n1] = jnp.dot(hb, w_ref[:, n0:n1], preferred_element_type=F32)


def _inproj_call(x2, sh1, sc1, w_re, seq):
    t, d = x2.shape
    n = w_re.shape[1]
    tpb = seq // INPROJ_TM
    vmem = 2 * INPROJ_TM * d * 4 + 2 * INPROJ_TM * n * 4 + d * n * 2 + 8 * 1024 * 1024
    return pl.pallas_call(
        _inproj_kernel,
        grid=(t // INPROJ_TM,),
        in_specs=[
            pl.BlockSpec((INPROJ_TM, d), lambda i: (i, 0)),
            pl.BlockSpec((1, 1, d), lambda i: (i // tpb, 0, 0)),
            pl.BlockSpec((1, 1, d), lambda i: (i // tpb, 0, 0)),
            pl.BlockSpec((d, n), lambda i: (0, 0), pipeline_mode=pl.Buffered(1)),
        ],
        out_specs=pl.BlockSpec((INPROJ_TM, n), lambda i: (i, 0)),
        out_shape=jax.ShapeDtypeStruct((t, n), F32),
        compiler_params=pltpu.CompilerParams(
            dimension_semantics=("arbitrary",), vmem_limit_bytes=_vmem_limit(vmem)),
        name="inproj",
    )(x2, sh1, sc1, w_re)


def _chunk_cumsum(x, row_in_chunk):
    s = 1
    while s < GLA_CHUNK:
        x = x + jnp.where(row_in_chunk >= s, pltpu.roll(x, s, 0), 0.0)
        s *= 2
    return x


def _group_ref(b, m, row):
    n, dk = b.shape
    g = 2 * m
    if g >= V7X_SUBLANES:
        r = b.reshape(n // g, g, dk)[:, m - 1:m, :]
        return jnp.broadcast_to(r, (n // g, g, dk)).reshape(n, dk)
    rr = row & (g - 1)
    out = None
    for r in range(g):
        sh = (r - (m - 1)) % n
        shifted = b if sh == 0 else pltpu.roll(b, sh, 0)
        out = shifted if out is None else jnp.where(rr == r, shifted, out)
    return out


def _dot_nt(a, b):
    return lax.dot_general(a, b, (((1,), (1,)), ((), ())), preferred_element_type=F32)


def _gla_kernel(q_ref, k_ref, v_ref, g_ref, lr_ref, wgk_ref, bgk_ref, nw_ref, o_ref, state_ref):
    n = q_ref.shape[0]
    dk, dv = GLA_HEAD_K, GLA_HEAD_V
    n_chunks = n // GLA_CHUNK

    @pl.when(pl.program_id(1) == 0)
    def _():
        state_ref[...] = jnp.zeros_like(state_ref)

    row = lax.broadcasted_iota(I32, (n, dk), 0)
    row_in_chunk = row & (GLA_CHUNK - 1)
    ti = lax.broadcasted_iota(I32, (n, n), 0)
    si = lax.broadcasted_iota(I32, (n, n), 1)
    x = ti ^ si
    level = jnp.zeros((n, n), I32)
    m = 1
    while m < GLA_CHUNK:
        level = level + (x >= m).astype(I32)
        m *= 2
    level = jnp.where((si <= ti) & (x < GLA_CHUNK), level, -1)

    z = jnp.dot(lr_ref[...].astype(BF16), wgk_ref[...], preferred_element_type=F32) + bgk_ref[...]
    log_alpha = (jnp.minimum(z, 0.0) - jnp.log(1.0 + jnp.exp(-jnp.abs(z)))) * (1.0 / GLA_GATE_TAU)

    scale = GLA_HEAD_K ** -0.5
    for h in range(GLA_HEADS):
        q = q_ref[:, h * dk:(h + 1) * dk] * scale
        k = k_ref[:, h * dk:(h + 1) * dk]
        vb = v_ref[:, h * dv:(h + 1) * dv].astype(BF16)
        bc = _chunk_cumsum(log_alpha[:, h * dk:(h + 1) * dk], row_in_chunk)

        scores = jnp.where(level == 0, _dot_nt(q.astype(BF16), k.astype(BF16)), 0.0)
        m = 1
        lvl = 1
        while m < GLA_CHUNK:
            e = jnp.exp(-jnp.abs(bc - _group_ref(bc, m, row)))
            s = _dot_nt((q * e).astype(BF16), (k * e).astype(BF16))
            scores = jnp.where(level == lvl, s, scores)
            m *= 2
            lvl += 1
        o = jnp.dot(scores.astype(BF16), vb, preferred_element_type=F32)

        qd = (q * jnp.exp(bc)).astype(BF16)
        inter = []
        for c in range(n_chunks):
            r0 = c * GLA_CHUNK
            r1 = r0 + GLA_CHUNK
            state = state_ref[h]
            inter.append(jnp.dot(qd[r0:r1], state.astype(BF16), preferred_element_type=F32))
            bcc = bc[r0:r1]
            blast = bcc[GLA_CHUNK - 1:GLA_CHUNK, :]
            kd_t = (k[r0:r1] * jnp.exp(blast - bcc)).T.astype(BF16)
            decay = jnp.exp(bcc.T[:, GLA_CHUNK - 1:GLA_CHUNK])
            state_ref[h] = decay * state + jnp.dot(kd_t, vb[r0:r1], preferred_element_type=F32)
        o = o + jnp.concatenate(inter, axis=0)

        o = o * lax.rsqrt(jnp.mean(o * o, axis=-1, keepdims=True) + GLA_NORM_EPS) * nw_ref[...]
        g = g_ref[:, h * dv:(h + 1) * dv]
        o_ref[:, h * dv:(h + 1) * dv] = (o * (g * jax.nn.sigmoid(g))).astype(o_ref.dtype)


def _gla_call(proj, wgk_pad, bgk, norm_w, batch, seq):
    t = proj.shape[0]
    nb = seq // GLA_BLOCK
    kw, vw = GLA_K_WIDTH, GLA_V_WIDTH
    q_col = POOL_WIDTH // kw
    v_col = (POOL_WIDTH + 2 * kw) // vw
    g_col = v_col + 1
    lr_col = (POOL_WIDTH + 2 * kw + 2 * vw) // V7X_LANES
    row = lambda b, j: b * nb + j
    return pl.pallas_call(
        _gla_kernel,
        grid=(batch, nb),
        in_specs=[
            pl.BlockSpec((GLA_BLOCK, kw), lambda b, j: (row(b, j), q_col)),
            pl.BlockSpec((GLA_BLOCK, kw), lambda b, j: (row(b, j), q_col + 1)),
            pl.BlockSpec((GLA_BLOCK, vw), lambda b, j: (row(b, j), v_col)),
            pl.BlockSpec((GLA_BLOCK, vw), lambda b, j: (row(b, j), g_col)),
            pl.BlockSpec((GLA_BLOCK, V7X_LANES), lambda b, j: (row(b, j), lr_col)),
            pl.BlockSpec((V7X_LANES, kw), lambda b, j: (0, 0)),
            pl.BlockSpec((1, kw), lambda b, j: (0, 0)),
            pl.BlockSpec((1, GLA_HEAD_V), lambda b, j: (0, 0)),
        ],
        out_specs=pl.BlockSpec((GLA_BLOCK, vw), lambda b, j: (row(b, j), 0)),
        out_shape=jax.ShapeDtypeStruct((t, vw), BF16),
        scratch_shapes=[pltpu.VMEM((GLA_HEADS, GLA_HEAD_K, GLA_HEAD_V), F32)],
        compiler_params=pltpu.CompilerParams(dimension_semantics=("arbitrary", "arbitrary")),
        name="gla",
    )(proj, proj, proj, proj, proj, wgk_pad, bgk, norm_w)


def _outproj_kernel(u_ref, yg_ref, x_ref, g1_ref, sh2_ref, sc2_ref, wp_ref, bp_ref, ps_ref, wo_ref,
                    l1g_ref, l1b_ref, wra_ref, wrb_ref, br_ref,
                    x1_ref, h2_ref, idx_ref, tw_ref, rank_ref, cnt_ref,
                    halo_ref, carry_ref, *, tiles_per_seq):
    i = pl.program_id(0)
    tm = u_ref.shape[0]
    seq_tile = i % tiles_per_seq

    @pl.when(seq_tile == 0)
    def _():
        halo_ref[...] = jnp.zeros_like(halo_ref)

    @pl.when(i == 0)
    def _():
        carry_ref[...] = jnp.zeros_like(carry_ref)

    gd = POOL_GROUP_DIM
    part_rows = tm // OUT_PARTS
    lane = lax.broadcasted_iota(I32, (part_rows, V7X_LANES), 1)
    lane_f = lane.astype(F32)
    neg_inf = jnp.float32(-jnp.inf)
    tri = (lax.broadcasted_iota(I32, (part_rows, part_rows), 1)
           < lax.broadcasted_iota(I32, (part_rows, part_rows), 0)).astype(BF16)

    def mix(r0, halo):
        rows = slice(r0, r0 + part_rows)
        u = u_ref[rows, :]
        pos1 = seq_tile * tm + r0 + lax.broadcasted_iota(I32, (part_rows, gd), 0) + 1
        mixed = []
        for gi, win in enumerate(POOL_WINDOWS):
            cols = slice(gi * gd, (gi + 1) * gd)
            ug = u[:, cols]
            ws = jnp.concatenate([halo[:, cols], ug], axis=0)
            s = 1
            while s < win:
                ws = ws + pltpu.roll(ws, s, 0)
                s *= 2
            count = jnp.minimum(pos1, win).astype(F32)
            pooled = ws[POOL_HALO:, :] / count - ug
            m = jnp.dot(pooled.astype(BF16), wp_ref[gi], preferred_element_type=F32)
            mixed.append(((m + bp_ref[:, cols]) * ps_ref[:, cols]).astype(BF16))
        cat = jnp.concatenate(mixed + [yg_ref[rows, :]], axis=1)
        return jnp.dot(cat, wo_ref[...], preferred_element_type=F32)

    def norms(r0, y):
        rows = slice(r0, r0 + part_rows)
        x1 = _layer_norm(DEEPNORM_ALPHA * x_ref[rows, :] + g1_ref[0] * y) * l1g_ref[...] + l1b_ref[...]
        x1_ref[rows, :] = x1
        h2 = _layer_norm(x1) * (1.0 + sc2_ref[0]) + sh2_ref[0]
        h2_ref[rows, :] = h2
        hb = h2.astype(BF16)
        return hb, (h2 - hb.astype(F32)).astype(BF16)

    def route(h_hi, h_lo):
        ra = jnp.dot(h_hi, wra_ref[...], preferred_element_type=F32)
        rb = jnp.dot(h_lo, wrb_ref[...], preferred_element_type=F32)
        return ra + pltpu.roll(ra, V7X_LANES - N_EXPERTS, 1) + rb + br_ref[...]

    def select(r0, logits, carry):
        rows = slice(r0, r0 + part_rows)
        work = jnp.where(lane < N_EXPERTS, logits, neg_inf)
        vals, idxs = [], []
        for _ in range(TOP_K):
            mx = jnp.max(work, axis=-1, keepdims=True)
            sel = jnp.min(jnp.where(work == mx, lane_f, float(V7X_LANES)), axis=-1,
                          keepdims=True).astype(I32)
            vals.append(mx)
            idxs.append(sel)
            work = jnp.where(lane == sel, neg_inf, work)
        exps = [jnp.exp(v - vals[0]) for v in vals]
        denom = exps[0] + exps[1] + exps[2] + exps[3]

        chosen = jnp.where((lane < N_EXPERTS) & (work == neg_inf), 1.0, 0.0)
        excl = jnp.dot(tri, chosen.astype(BF16), preferred_element_type=F32) + carry
        idx_out = jnp.zeros((part_rows, V7X_LANES), I32)
        tw_out = jnp.zeros((part_rows, V7X_LANES), F32)
        rank_out = jnp.zeros((part_rows, V7X_LANES), F32)
        for kk in range(TOP_K):
            rk = jnp.sum(jnp.where(lane == idxs[kk], excl, 0.0), axis=-1, keepdims=True)
            idx_out = jnp.where(lane == kk, idxs[kk], idx_out)
            tw_out = jnp.where(lane == kk, exps[kk] / denom, tw_out)
            rank_out = jnp.where(lane == kk, rk, rank_out)
        idx_ref[rows, :] = idx_out
        tw_ref[rows, :] = tw_out
        rank_ref[rows, :] = rank_out.astype(I32)
        return carry + jnp.sum(chosen, axis=0, keepdims=True)

    starts = [p * part_rows for p in range(OUT_PARTS)]
    halos = [halo_ref[...]] + [u_ref[r0 - POOL_HALO:r0, :] for r0 in starts[1:]]
    ys = [mix(r0, halo) for r0, halo in zip(starts, halos)]
    carry = carry_ref[0:1, :]
    pending = None
    for r0, y in zip(starts, ys):
        h_hi, h_lo = norms(r0, y)
        if pending is not None:
            carry = select(*pending, carry)
        pending = (r0, route(h_hi, h_lo))
    carry = select(*pending, carry)
    halo_ref[...] = u_ref[tm - POOL_HALO:tm, :]
    carry_ref[...] = jnp.broadcast_to(carry, carry_ref.shape)
    cnt_ref[...] = jnp.broadcast_to(carry, cnt_ref.shape)


def _outproj_call(proj, y_gla, x2, g1, sh2, sc2, wp, bp, ps, wo, l1g, l1b, wra, wrb, br, seq):
    t, d = x2.shape
    tm = OUT_TM
    tps = seq // tm
    bidx = lambda i: (i // tps, 0, 0)
    const2 = lambda i: (0, 0)
    row = lambda i: (i, 0)
    lanes = V7X_LANES
    vmem = (2 * tm * (POOL_WIDTH * 4 + GLA_V_WIDTH * 2 + d * 4 + d * 4 + d * 4 + 3 * lanes * 4)
            + d * d * 2 + 16 * 1024 * 1024)
    return pl.pallas_call(
        functools.partial(_outproj_kernel, tiles_per_seq=tps),
        grid=(t // tm,),
        in_specs=[
            pl.BlockSpec((tm, POOL_WIDTH), row),
            pl.BlockSpec((tm, GLA_V_WIDTH), row),
            pl.BlockSpec((tm, d), row),
            pl.BlockSpec((1, 1, d), bidx),
            pl.BlockSpec((1, 1, d), bidx),
            pl.BlockSpec((1, 1, d), bidx),
            pl.BlockSpec((POOL_GROUPS, POOL_GROUP_DIM, POOL_GROUP_DIM), lambda i: (0, 0, 0)),
            pl.BlockSpec((1, POOL_WIDTH), const2),
            pl.BlockSpec((1, POOL_WIDTH), const2),
            pl.BlockSpec((d, d), const2, pipeline_mode=pl.Buffered(1)),
            pl.BlockSpec((1, d), const2),
            pl.BlockSpec((1, d), const2),
            pl.BlockSpec((d, lanes), const2),
            pl.BlockSpec((d, lanes), const2),
            pl.BlockSpec((1, lanes), const2),
        ],
        out_specs=[
            pl.BlockSpec((tm, d), row),
            pl.BlockSpec((tm, d), row),
            pl.BlockSpec((tm, lanes), row),
            pl.BlockSpec((tm, lanes), row),
            pl.BlockSpec((tm, lanes), row),
            pl.BlockSpec((V7X_SUBLANES, lanes), const2),
        ],
        out_shape=[
            jax.ShapeDtypeStruct((t, d), F32),
            jax.ShapeDtypeStruct((t, d), F32),
            jax.ShapeDtypeStruct((t, lanes), I32),
            jax.ShapeDtypeStruct((t, lanes), F32),
            jax.ShapeDtypeStruct((t, lanes), I32),
            jax.ShapeDtypeStruct((V7X_SUBLANES, lanes), F32),
        ],
        scratch_shapes=[
            pltpu.VMEM((POOL_HALO, POOL_WIDTH), F32),
            pltpu.VMEM((V7X_SUBLANES, lanes), F32),
        ],
        compiler_params=pltpu.CompilerParams(
            dimension_semantics=("arbitrary",), vmem_limit_bytes=_vmem_limit(vmem)),
        name="outproj",
    )(proj, y_gla, x2, g1, sh2, sc2, wp, bp, ps, wo, l1g, l1b, wra, wrb, br)


def _dispatch_kernel(tbl_ref, pos_ref, h_ref, xs_ref, zero_ref, sem, zsem):
    tm = h_ref.shape[0]
    n_experts = tbl_ref.shape[1]
    n_tiles = xs_ref.shape[0] // MOE_TM

    def pad_copy(p):
        return pltpu.make_async_copy(zero_ref.at[pl.ds(0, 1)], xs_ref.at[pl.ds(p, 1)], zsem)

    def tail_copy(tile):
        dst = xs_ref.at[pl.ds(pl.multiple_of(tile * MOE_TM, MOE_TM), MOE_TM)]
        return pltpu.make_async_copy(zero_ref, dst, zsem)

    def for_each_pad(fn):
        def per_expert(e, c):
            cnt = tbl_ref[0, e]
            first = tbl_ref[1, e] + cnt
            n_pad = (cnt + MOE_TM - 1) // MOE_TM * MOE_TM - cnt

            def per_row(r, c2):
                fn(pad_copy(first + r))
                return c2
            lax.fori_loop(0, n_pad, per_row, 0)
            return c
        lax.fori_loop(0, n_experts, per_expert, 0)
        last = n_experts - 1
        tail0 = (tbl_ref[1, last] + tbl_ref[0, last] + MOE_TM - 1) // MOE_TM

        def per_tile(r, c):
            fn(tail_copy(tail0 + r))
            return c
        lax.fori_loop(0, n_tiles - tail0, per_tile, 0)

    @pl.when(pl.program_id(0) == 0)
    def _():
        zero_ref[...] = jnp.zeros_like(zero_ref)
        for_each_pad(lambda cp: cp.start())
        for_each_pad(lambda cp: cp.wait())

    def row_copy(t, p):
        return pltpu.make_async_copy(h_ref.at[pl.ds(t, 1)], xs_ref.at[pl.ds(p, 1)], sem)

    def issue(t, carry):
        for kk in range(TOP_K):
            row_copy(t, pos_ref[0, 0, t * TOP_K + kk]).start(priority=kk % 2)
        return carry

    lax.fori_loop(0, tm, issue, 0, unroll=2)
    for kk in range(TOP_K):
        pltpu.make_async_copy(h_ref, xs_ref.at[pl.ds(0, tm)], sem).wait()


def _dispatch_call(tbl, pos, h2, n_slots):
    t, w = h2.shape
    tm = DISPATCH_TM
    pos3 = pos.reshape(t // tm, 1, tm * TOP_K)
    return pl.pallas_call(
        _dispatch_kernel,
        grid=(t // tm,),
        in_specs=[
            pl.BlockSpec(memory_space=pltpu.SMEM),
            pl.BlockSpec((1, 1, tm * TOP_K), lambda i: (i, 0, 0), memory_space=pltpu.SMEM),
            pl.BlockSpec((tm, w), lambda i: (i, 0)),
        ],
        out_specs=pl.BlockSpec(memory_space=pl.ANY),
        out_shape=jax.ShapeDtypeStruct((n_slots, w), h2.dtype),
        scratch_shapes=[
            pltpu.VMEM((MOE_TM, w), h2.dtype),
            pltpu.SemaphoreType.DMA(()),
            pltpu.SemaphoreType.DMA(()),
        ],
        compiler_params=pltpu.CompilerParams(dimension_semantics=("arbitrary",)),
        name="dispatch",
    )(tbl, pos3, h2)


def _moe_kernel(item_e_ref, item_row_ref, item_n_ref, tail_ref,
                xs_ref, wg_ref, wu_ref, wd_ref, bg_ref, bu_ref, bd_ref,
                ys_ref,
                xbuf, stage, yacc, wgu_bf, wd_bf, sem_in, sem_out):
    del item_e_ref
    i = pl.program_id(0)
    j = pl.program_id(1)
    n_items = pl.num_programs(0)
    n_chunks = pl.num_programs(1)
    nsub = item_n_ref[i]
    row0 = item_row_ref[i]
    tm, tf = MOE_TM, MOE_TF

    def in_copy(r):
        row = jnp.minimum(row0 + r * tm, xs_ref.shape[0] - tm)
        src = xs_ref.at[pl.ds(pl.multiple_of(row, tm), tm)]
        return pltpu.make_async_copy(src, stage.at[r & 1], sem_in.at[r & 1])

    def out_copy(r):
        dst = ys_ref.at[pl.ds(pl.multiple_of(row0 + r * tm, tm), tm)]
        return pltpu.make_async_copy(yacc.at[pl.ds(pl.multiple_of(r * tm, tm), tm)], dst, sem_out.at[r])

    @pl.when((j == 0) & (nsub > 0))
    def _():
        in_copy(0).start()
        in_copy(1).start()

    def chunk(first, last):
        wgu_bf[:, :tf] = wg_ref[0].astype(BF16)
        wgu_bf[:, tf:] = wu_ref[0].astype(BF16)
        wd_bf[...] = wd_ref[0].astype(BF16)

        def activation(r):
            rows = pl.ds(pl.multiple_of(r * tm, tm), tm)
            if first:
                in_copy(r).wait()
                xb = stage[r & 1].astype(BF16)
                xbuf[rows, :] = xb
                in_copy(r + 2).start()
            else:
                xb = xbuf[rows, :]
            gu = jnp.dot(xb, wgu_bf[...], preferred_element_type=F32)
            gate = jnp.minimum(gu[:, :tf] + bg_ref[0], SWIGLU_LIMIT)
            up = jnp.clip(gu[:, tf:] + bu_ref[0], -SWIGLU_LIMIT, SWIGLU_LIMIT)
            return (gate * jax.nn.sigmoid(SWIGLU_ALPHA * gate) * (up + 1.0)).astype(BF16)

        def down(r, act):
            rows = pl.ds(pl.multiple_of(r * tm, tm), tm)
            part = jnp.dot(act, wd_bf[...], preferred_element_type=F32)
            if first:
                yacc[rows, :] = part + bd_ref[0]
            else:
                yacc[rows, :] = yacc[rows, :] + part
            if last:
                out_copy(r).start()

        def step(r, act):
            nxt = activation(r + 1)
            down(r, act)
            return nxt

        def steps_from(r, count, act):
            for c in range(count):
                act = step(r + c, act)
            return act

        n_steps = nsub - 1
        n_full = n_steps // MOE_UNROLL
        act = lax.fori_loop(0, n_full, lambda p, a: steps_from(MOE_UNROLL * p, MOE_UNROLL, a),
                            activation(0))
        done = n_full * MOE_UNROLL
        rem = n_steps - done
        act = lax.cond(rem >= 2, lambda a: steps_from(done, 2, a), lambda a: a, act)
        done = done + jnp.where(rem >= 2, 2, 0)
        act = lax.cond(rem % 2 == 1, lambda a: step(done, a), lambda a: a, act)
        down(nsub - 1, act)

        if first:
            in_copy(0).wait()
            in_copy(1).wait()
        if last:
            def drain(r, c):
                out_copy(r).wait()
                return c
            lax.fori_loop(0, nsub, drain, 0)

    @pl.when((nsub > 0) & (j == 0))
    def _():
        chunk(True, False)

    @pl.when((nsub > 0) & (j > 0) & (j < n_chunks - 1))
    def _():
        chunk(False, False)

    @pl.when((nsub > 0) & (j == n_chunks - 1))
    def _():
        chunk(False, True)

    @pl.when((i == n_items - 1) & (j == n_chunks - 1))
    def _():
        yacc[0:tm, :] = jnp.zeros((tm, D_MODEL), F32)
        tail0 = tail_ref[0]
        n_tail = ys_ref.shape[0] // tm - tail0

        def tail_copy(r):
            dst = ys_ref.at[pl.ds(pl.multiple_of((tail0 + r) * tm, tm), tm)]
            return pltpu.make_async_copy(yacc.at[0:tm], dst, sem_out.at[0])

        def start(r, c):
            tail_copy(r).start()
            return c

        def drain(r, c):
            tail_copy(r).wait()
            return c

        lax.fori_loop(0, n_tail, start, 0)
        lax.fori_loop(0, n_tail, drain, 0)


def _moe_call(item_e, item_row, item_n, tail_tile, xs, wg, wu, wd, bg, bu, bd):
    n_slots, d = xs.shape
    e, _, f = wg.shape
    n_items = item_e.shape[0]
    n_chunks = f // MOE_TF
    rows = MOE_SUB * MOE_TM

    def jeff(i, j, n_ref):
        return jnp.where(n_ref[i] > 0, j, n_chunks - 1)

    vmem = (rows * d * 2 + 2 * MOE_TM * d * 4 + rows * d * 4 + 3 * 2 * d * MOE_TF * 4
            + 3 * d * MOE_TF * 2 + 6 * 1024 * 1024)
    grid_spec = pltpu.PrefetchScalarGridSpec(
        num_scalar_prefetch=4,
        grid=(n_items, n_chunks),
        in_specs=[
            pl.BlockSpec(memory_space=pl.ANY),
            pl.BlockSpec((1, d, MOE_TF), lambda i, j, e_r, r_r, n_r, t_r: (e_r[i], 0, jeff(i, j, n_r))),
            pl.BlockSpec((1, d, MOE_TF), lambda i, j, e_r, r_r, n_r, t_r: (e_r[i], 0, jeff(i, j, n_r))),
            pl.BlockSpec((1, MOE_TF, d), lambda i, j, e_r, r_r, n_r, t_r: (e_r[i], jeff(i, j, n_r), 0)),
            pl.BlockSpec((1, 1, MOE_TF), lambda i, j, e_r, r_r, n_r, t_r: (e_r[i], 0, jeff(i, j, n_r))),
            pl.BlockSpec((1, 1, MOE_TF), lambda i, j, e_r, r_r, n_r, t_r: (e_r[i], 0, jeff(i, j, n_r))),
            pl.BlockSpec((1, 1, d), lambda i, j, e_r, r_r, n_r, t_r: (e_r[i], 0, 0)),
        ],
        out_specs=pl.BlockSpec(memory_space=pl.ANY),
        scratch_shapes=[
            pltpu.VMEM((rows, d), BF16),
            pltpu.VMEM((2, MOE_TM, d), F32),
            pltpu.VMEM((rows, d), F32),
            pltpu.VMEM((d, 2 * MOE_TF), BF16),
            pltpu.VMEM((MOE_TF, d), BF16),
            pltpu.SemaphoreType.DMA((2,)),
            pltpu.SemaphoreType.DMA((MOE_SUB,)),
        ],
    )
    return pl.pallas_call(
        _moe_kernel,
        grid_spec=grid_spec,
        out_shape=jax.ShapeDtypeStruct((n_slots, d), F32),
        compiler_params=pltpu.CompilerParams(
            dimension_semantics=("arbitrary", "arbitrary"), vmem_limit_bytes=_vmem_limit(vmem)),
        name="moe",
    )(item_e, item_row, item_n, tail_tile, xs, wg, wu, wd, bg.reshape(e, 1, f), bu.reshape(e, 1, f),
      bd.reshape(e, 1, d))


def _combine_kernel(pos_ref, posn_ref, ys_ref, x1_ref, tw_ref, g2_ref, l2g_ref, l2b_ref,
                    o_ref, buf, sem):
    i = pl.program_id(0)
    n = pl.num_programs(0)
    tm = x1_ref.shape[0]
    slot = i % 2

    def row_copy(p_ref, t, kk, s):
        p = p_ref[0, 0, t * TOP_K + kk]
        return pltpu.make_async_copy(ys_ref.at[pl.ds(p, 1)], buf.at[s, pl.ds(kk * tm + t, 1)], sem.at[s])

    def issue(p_ref, s):
        def body(t, c):
            for kk in range(TOP_K):
                row_copy(p_ref, t, kk, s).start(priority=kk % 2)
            return c
        lax.fori_loop(0, tm, body, 0, unroll=2)

    @pl.when(i == 0)
    def _():
        issue(pos_ref, 0)

    for s in range(2):
        @pl.when((i + 1 < n) & (slot == 1 - s))
        def _():
            issue(posn_ref, s)

    pltpu.make_async_copy(ys_ref.at[pl.ds(0, TOP_K * tm)], buf.at[slot], sem.at[slot]).wait()

    tw = tw_ref[...]
    y = None
    for kk in range(TOP_K):
        part = tw[:, kk:kk + 1] * buf[slot, kk * tm:(kk + 1) * tm, :]
        y = part if y is None else y + part
    r = DEEPNORM_ALPHA * x1_ref[...] + g2_ref[0] * y
    o_ref[...] = _layer_norm(r) * l2g_ref[...] + l2b_ref[...]


def _combine_call(pos, ys, x1, tw, g2, l2g, l2b, seq):
    t, d = x1.shape
    tm = COMBINE_TM
    n = t // tm
    tps = seq // tm
    pos3 = pos.reshape(n, 1, tm * TOP_K)
    lanes = V7X_LANES
    vmem = 2 * TOP_K * tm * d * 4 + 4 * tm * d * 4 + 8 * 1024 * 1024
    return pl.pallas_call(
        _combine_kernel,
        grid=(n,),
        in_specs=[
            pl.BlockSpec((1, 1, tm * TOP_K), lambda i: (i, 0, 0), memory_space=pltpu.SMEM),
            pl.BlockSpec((1, 1, tm * TOP_K), lambda i: (jnp.minimum(i + 1, n - 1), 0, 0),
                         memory_space=pltpu.SMEM),
            pl.BlockSpec(memory_space=pl.ANY),
            pl.BlockSpec((tm, d), lambda i: (i, 0)),
            pl.BlockSpec((tm, lanes), lambda i: (i, 0)),
            pl.BlockSpec((1, 1, d), lambda i: (i // tps, 0, 0)),
            pl.BlockSpec((1, d), lambda i: (0, 0)),
            pl.BlockSpec((1, d), lambda i: (0, 0)),
        ],
        out_specs=pl.BlockSpec((tm, d), lambda i: (i, 0)),
        out_shape=jax.ShapeDtypeStruct((t, d), F32),
        scratch_shapes=[
            pltpu.VMEM((2, TOP_K * tm, d), F32),
            pltpu.SemaphoreType.DMA((2,)),
        ],
        compiler_params=pltpu.CompilerParams(
            dimension_semantics=("arbitrary",), vmem_limit_bytes=_vmem_limit(vmem)),
        name="combine",
    )(pos3, pos3, ys, x1, tw, g2, l2g, l2b)


def _routing_tables(counts, idx4, rank4, n_items):
    e = counts.shape[0]
    padded = (counts + MOE_TM - 1) // MOE_TM * MOE_TM
    pad_end = jnp.cumsum(padded)
    pad_start = pad_end - padded
    pos = pad_start[idx4] + rank4

    nsub_e = padded // MOE_TM
    items_e = (nsub_e + MOE_SUB - 1) // MOE_SUB
    item_end = jnp.cumsum(items_e)
    item_start = item_end - items_e
    total = item_end[-1]
    w = jnp.arange(n_items, dtype=I32)
    last = jnp.maximum(total - 1, 0)
    wv = jnp.minimum(w, last)
    ew = jnp.minimum(jnp.sum((item_end[None, :] <= wv[:, None]).astype(I32), axis=1), e - 1)
    s = wv - item_start[ew]
    valid = w < total
    item_row = jnp.where(valid, pad_start[ew] + s * (MOE_SUB * MOE_TM), 0).astype(I32)
    item_n = jnp.where(valid, jnp.clip(nsub_e[ew] - s * MOE_SUB, 0, MOE_SUB), 0).astype(I32)
    tail_tile = (pad_end[-1:] // MOE_TM).astype(I32)
    group_tbl = jnp.stack([counts, pad_start]).astype(I32)
    return pos.astype(I32), ew, item_row, item_n, tail_tile, group_tbl


def kernel(x, c, w_ada, b_ada, w_in, w_gk, b_gk, w_pool, b_pool, pool_scale, gla_norm_w, w_out,
           ln1_g, ln1_b, w_router, b_router, w_gate, b_gate, w_up, b_up, w_down, b_down,
           ln2_g, ln2_b):
    batch, seq, d = x.shape
    t = batch * seq
    e = w_router.shape[-1]
    lanes = V7X_LANES
    for l in range(w_ada.shape[0]):
        mod = _mod_call(c, w_ada[l], b_ada[l])
        sh1, sc1, g1, sh2, sc2, g2 = [m.reshape(batch, 1, d) for m in jnp.split(mod, 6, axis=-1)]

        wi = w_in[l]
        lr0 = POOL_WIDTH + 2 * GLA_K_WIDTH + GLA_V_WIDTH
        w_re = jnp.concatenate(
            [wi[:, :lr0], wi[:, lr0 + GLA_GATE_RANK:], wi[:, lr0:lr0 + GLA_GATE_RANK],
             jnp.zeros((d, lanes - GLA_GATE_RANK), wi.dtype)], axis=1).astype(BF16)
        x2 = x.reshape(t, d)
        proj = _inproj_call(x2, sh1, sc1, w_re, seq)

        wgk_pad = jnp.concatenate(
            [w_gk[l], jnp.zeros((lanes - GLA_GATE_RANK, GLA_K_WIDTH), w_gk.dtype)], axis=0).astype(BF16)
        y_gla = _gla_call(proj, wgk_pad, b_gk[l].reshape(1, -1), gla_norm_w[l].reshape(1, -1), batch, seq)

        wr = w_router[l]
        wr_hi = wr.astype(BF16)
        wr_lo = (wr - wr_hi.astype(F32)).astype(BF16)
        wra = jnp.concatenate([wr_hi, wr_lo, jnp.zeros((d, lanes - 2 * e), BF16)], axis=1)
        wrb = jnp.concatenate([wr_hi, jnp.zeros((d, lanes - e), BF16)], axis=1)
        br = jnp.concatenate([b_router[l], jnp.zeros((lanes - e,), F32)]).reshape(1, lanes)
        x1, h2, idx, tw, rank, cnt = _outproj_call(
            proj, y_gla, x2, g1, sh2, sc2,
            w_pool[l].astype(BF16), b_pool[l].reshape(1, -1), pool_scale[l].reshape(1, -1),
            w_out[l].astype(BF16), ln1_g[l].reshape(1, -1), ln1_b[l].reshape(1, -1),
            wra, wrb, br, seq)

        n_slots = t * TOP_K + e * MOE_TM
        n_items = e + n_slots // (MOE_SUB * MOE_TM)
        counts = cnt[0, :e].astype(I32)
        pos, item_e, item_row, item_n, tail_tile, group_tbl = _routing_tables(
            counts, idx[:, :TOP_K], rank[:, :TOP_K], n_items)

        xs = _dispatch_call(group_tbl, pos, h2, n_slots)
        ys = _moe_call(item_e, item_row, item_n, tail_tile, xs, w_gate[l], w_up[l], w_down[l],
                       b_gate[l], b_up[l], b_down[l])
        out = _combine_call(pos, ys, x1, tw, g2, ln2_g[l].reshape(1, -1), ln2_b[l].reshape(1, -1), seq)
        x = out.reshape(batch, seq, d)
    return x
```

```python
import functools

import jax
import jax.numpy as jnp
from jax import lax
from jax.experimental import pallas as pl
from jax.experimental.pallas import tpu as pltpu

F32 = jnp.float32
BF16 = jnp.bfloat16
I32 = jnp.int32

D_MODEL = 2048
POOL_WIDTH = 1024
POOL_GROUPS = 4
POOL_GROUP_DIM = 256
POOL_WINDOWS = (2, 4, 8, 16)
POOL_HALO = 16
GLA_HEADS = 4
GLA_HEAD_K = 128
GLA_HEAD_V = 256
GLA_K_WIDTH = GLA_HEADS * GLA_HEAD_K
GLA_V_WIDTH = GLA_HEADS * GLA_HEAD_V
GLA_GATE_RANK = 16
GLA_GATE_TAU = 16.0
GLA_CHUNK = 64
GLA_NORM_EPS = 1e-6
N_EXPERTS = 32
TOP_K = 4
D_FF = 2048
SWIGLU_ALPHA = 1.702
SWIGLU_LIMIT = 7.0
DEPTH = 1
DEEPNORM_ALPHA = (2.0 * DEPTH) ** 0.25
LN_EPS = 1e-5

V7X_LANES = 128
V7X_SUBLANES = 8
V7X_VMEM_BYTES = 64 * 1024 * 1024

MOD_TN = 512
MOD_KC = 64
INPROJ_TM = 512
INPROJ_TN = 512
PROJ_WIDTH = 4096 + V7X_LANES
GLA_BLOCK = 256
OUT_TM = 512
OUT_PARTS = 2
DISPATCH_TM = 512
MOE_TM = 256
MOE_SUB = 9
MOE_TF = 256
MOE_UNROLL = 4
COMBINE_TM = 256


def _vmem_limit(nbytes):
    return int(min(nbytes, V7X_VMEM_BYTES - 4 * 1024 * 1024))


def _layer_norm(x):
    mu = jnp.mean(x, axis=-1, keepdims=True)
    xc = x - mu
    var = jnp.mean(xc * xc, axis=-1, keepdims=True)
    return xc * lax.rsqrt(var + LN_EPS)


def _mod_kernel(c_ref, w_ref, b_ref, o_ref, s_ref):
    nb = c_ref.shape[0]
    d, tn = w_ref.shape
    rep = tn // V7X_LANES
    g = MOD_KC // V7X_SUBLANES

    @pl.when(pl.program_id(0) == 0)
    def _():
        for b in range(nb):
            cb = c_ref[b]
            s_ref[b] = cb * jax.nn.sigmoid(cb)

    def body(i, accs):
        k0 = pl.multiple_of(i * MOD_KC, MOD_KC)
        w = w_ref[pl.ds(k0, MOD_KC), :].reshape(g, V7X_SUBLANES, tn)
        out = []
        for b in range(nb):
            s = s_ref[b, pl.ds(k0, MOD_KC), :]
            s = jnp.concatenate([s] * rep, axis=1).reshape(g, V7X_SUBLANES, tn)
            out.append(accs[b] + jnp.sum(w * s, axis=0))
        return tuple(out)

    init = tuple(jnp.zeros((V7X_SUBLANES, tn), F32) for _ in range(nb))
    accs = lax.fori_loop(0, d // MOD_KC, body, init)
    rows = [jnp.sum(a, axis=0, keepdims=True) for a in accs]
    rows.append(jnp.zeros((V7X_SUBLANES - nb, tn), F32))
    o_ref[...] = jnp.concatenate(rows, axis=0) + b_ref[...]


def _mod_call(c, w_ada, b_ada):
    nb, d = c.shape
    n = w_ada.shape[1]
    c_rep = jnp.broadcast_to(c[:, :, None], (nb, d, V7X_LANES))
    out = pl.pallas_call(
        _mod_kernel,
        grid=(n // MOD_TN,),
        in_specs=[
            pl.BlockSpec((nb, d, V7X_LANES), lambda j: (0, 0, 0)),
            pl.BlockSpec((d, MOD_TN), lambda j: (0, j)),
            pl.BlockSpec((1, MOD_TN), lambda j: (0, j)),
        ],
        out_specs=pl.BlockSpec((V7X_SUBLANES, MOD_TN), lambda j: (0, j)),
        out_shape=jax.ShapeDtypeStruct((V7X_SUBLANES, n), F32),
        scratch_shapes=[pltpu.VMEM((nb, d, V7X_LANES), F32)],
        compiler_params=pltpu.CompilerParams(dimension_semantics=("arbitrary",)),
        name="mod",
    )(c_rep, w_ada, b_ada.reshape(1, n))
    return out[:nb]


def _inproj_kernel(x_ref, sh_ref, sc_ref, w_ref, o_ref):
    h = _layer_norm(x_ref[...]) * (1.0 + sc_ref[0]) + sh_ref[0]
    hb = h.astype(BF16)
    n = w_ref.shape[1]
    for n0 in range(0, n, INPROJ_TN):
        n1 = min(n0 + INPROJ_TN, n)
        o_ref[:, n0:n1] = jnp.dot(hb, w_ref[:, n0:n1], preferred_element_type=F32)


def _inproj_call(x2, sh1, sc1, w_re, seq):
    t, d = x2.shape
    n = w_re.shape[1]
    tpb = seq // INPROJ_TM
    vmem = 2 * INPROJ_TM * d * 4 + 2 * INPROJ_TM * n * 4 + d * n * 2 + 8 * 1024 * 1024
    return pl.pallas_call(
        _inproj_kernel,
        grid=(t // INPROJ_TM,),
        in_specs=[
            pl.BlockSpec((INPROJ_TM, d), lambda i: (i, 0)),
            pl.BlockSpec((1, 1, d), lambda i: (i // tpb, 0, 0)),
            pl.BlockSpec((1, 1, d), lambda i: (i // tpb, 0, 0)),
            pl.BlockSpec((d, n), lambda i: (0, 0), pipeline_mode=pl.Buffered(1)),
        ],
        out_specs=pl.BlockSpec((INPROJ_TM, n), lambda i: (i, 0)),
        out_shape=jax.ShapeDtypeStruct((t, n), F32),
        compiler_params=pltpu.CompilerParams(
            dimension_semantics=("arbitrary",), vmem_limit_bytes=_vmem_limit(vmem)),
        name="inproj",
    )(x2, sh1, sc1, w_re)


def _chunk_cumsum(x, row_in_chunk):
    s = 1
    while s < GLA_CHUNK:
        x = x + jnp.where(row_in_chunk >= s, pltpu.roll(x, s, 0), 0.0)
        s *= 2
    return x


def _group_ref(b, m, row):
    n, dk = b.shape
    g = 2 * m
    if g >= V7X_SUBLANES:
        r = b.reshape(n // g, g, dk)[:, m - 1:m, :]
        return jnp.broadcast_to(r, (n // g, g, dk)).reshape(n, dk)
    rr = row & (g - 1)
    out = None
    for r in range(g):
        sh = (r - (m - 1)) % n
        shifted = b if sh == 0 else pltpu.roll(b, sh, 0)
        out = shifted if out is None else jnp.where(rr == r, shifted, out)
    return out


def _dot_nt(a, b):
    return lax.dot_general(a, b, (((1,), (1,)), ((), ())), preferred_element_type=F32)


def _gla_kernel(q_ref, k_ref, v_ref, g_ref, lr_ref, wgk_ref, bgk_ref, nw_ref, o_ref, state_ref):
    n = q_ref.shape[0]
    dk, dv = GLA_HEAD_K, GLA_HEAD_V
    n_chunks = n // GLA_CHUNK

    @pl.when(pl.program_id(1) == 0)
    def _():
        state_ref[...] = jnp.zeros_like(state_ref)

    row = lax.broadcasted_iota(I32, (n, dk), 0)
    row_in_chunk = row & (GLA_CHUNK - 1)
    ti = lax.broadcasted_iota(I32, (n, n), 0)
    si = lax.broadcasted_iota(I32, (n, n), 1)
    x = ti ^ si
    level = jnp.zeros((n, n), I32)
    m = 1
    while m < GLA_CHUNK:
        level = level + (x >= m).astype(I32)
        m *= 2
    level = jnp.where((si <= ti) & (x < GLA_CHUNK), level, -1)

    z = jnp.dot(lr_ref[...].astype(BF16), wgk_ref[...], preferred_element_type=F32) + bgk_ref[...]
    log_alpha = (jnp.minimum(z, 0.0) - jnp.log(1.0 + jnp.exp(-jnp.abs(z)))) * (1.0 / GLA_GATE_TAU)

    scale = GLA_HEAD_K ** -0.5
    for h in range(GLA_HEADS):
        q = q_ref[:, h * dk:(h + 1) * dk] * scale
        k = k_ref[:, h * dk:(h + 1) * dk]
        vb = v_ref[:, h * dv:(h + 1) * dv].astype(BF16)
        bc = _chunk_cumsum(log_alpha[:, h * dk:(h + 1) * dk], row_in_chunk)

        scores = jnp.where(level == 0, _dot_nt(q.astype(BF16), k.astype(BF16)), 0.0)
        m = 1
        lvl = 1
        while m < GLA_CHUNK:
            e = jnp.exp(-jnp.abs(bc - _group_ref(bc, m, row)))
            s = _dot_nt((q * e).astype(BF16), (k * e).astype(BF16))
            scores = jnp.where(level == lvl, s, scores)
            m *= 2
            lvl += 1
        o = jnp.dot(scores.astype(BF16), vb, preferred_element_type=F32)

        qd = (q * jnp.exp(bc)).astype(BF16)
        inter = []
        for c in range(n_chunks):
            r0 = c * GLA_CHUNK
            r1 = r0 + GLA_CHUNK
            state = state_ref[h]
            inter.append(jnp.dot(qd[r0:r1], state.astype(BF16), preferred_element_type=F32))
            bcc = bc[r0:r1]
            blast = bcc[GLA_CHUNK - 1:GLA_CHUNK, :]
            kd_t = (k[r0:r1] * jnp.exp(blast - bcc)).T.astype(BF16)
            decay = jnp.exp(bcc.T[:, GLA_CHUNK - 1:GLA_CHUNK])
            state_ref[h] = decay * state + jnp.dot(kd_t, vb[r0:r1], preferred_element_type=F32)
        o = o + jnp.concatenate(inter, axis=0)

        o = o * lax.rsqrt(jnp.mean(o * o, axis=-1, keepdims=True) + GLA_NORM_EPS) * nw_ref[...]
        g = g_ref[:, h * dv:(h + 1) * dv]
        o_ref[:, h * dv:(h + 1) * dv] = (o * (g * jax.nn.sigmoid(g))).astype(o_ref.dtype)


def _gla_call(proj, wgk_pad, bgk, norm_w, batch, seq):
    t = proj.shape[0]
    nb = seq // GLA_BLOCK
    kw, vw = GLA_K_WIDTH, GLA_V_WIDTH
    q_col = POOL_WIDTH // kw
    v_col = (POOL_WIDTH + 2 * kw) // vw
    g_col = v_col + 1
    lr_col = (POOL_WIDTH + 2 * kw + 2 * vw) // V7X_LANES
    row = lambda b, j: b * nb + j
    return pl.pallas_call(
        _gla_kernel,
        grid=(batch, nb),
        in_specs=[
            pl.BlockSpec((GLA_BLOCK, kw), lambda b, j: (row(b, j), q_col)),
            pl.BlockSpec((GLA_BLOCK, kw), lambda b, j: (row(b, j), q_col + 1)),
            pl.BlockSpec((GLA_BLOCK, vw), lambda b, j: (row(b, j), v_col)),
            pl.BlockSpec((GLA_BLOCK, vw), lambda b, j: (row(b, j), g_col)),
            pl.BlockSpec((GLA_BLOCK, V7X_LANES), lambda b, j: (row(b, j), lr_col)),
            pl.BlockSpec((V7X_LANES, kw), lambda b, j: (0, 0)),
            pl.BlockSpec((1, kw), lambda b, j: (0, 0)),
            pl.BlockSpec((1, GLA_HEAD_V), lambda b, j: (0, 0)),
        ],
        out_specs=pl.BlockSpec((GLA_BLOCK, vw), lambda b, j: (row(b, j), 0)),
        out_shape=jax.ShapeDtypeStruct((t, vw), BF16),
        scratch_shapes=[pltpu.VMEM((GLA_HEADS, GLA_HEAD_K, GLA_HEAD_V), F32)],
        compiler_params=pltpu.CompilerParams(dimension_semantics=("arbitrary", "arbitrary")),
        name="gla",
    )(proj, proj, proj, proj, proj, wgk_pad, bgk, norm_w)


def _outproj_kernel(u_ref, yg_ref, x_ref, g1_ref, sh2_ref, sc2_ref, wp_ref, bp_ref, ps_ref, wo_ref,
                    l1g_ref, l1b_ref, wra_ref, wrb_ref, br_ref,
                    x1_ref, h2_ref, idx_ref, tw_ref, rank_ref, cnt_ref,
                    halo_ref, carry_ref, *, tiles_per_seq):
    i = pl.program_id(0)
    tm = u_ref.shape[0]
    seq_tile = i % tiles_per_seq

    @pl.when(seq_tile == 0)
    def _():
        halo_ref[...] = jnp.zeros_like(halo_ref)

    @pl.when(i == 0)
    def _():
        carry_ref[...] = jnp.zeros_like(carry_ref)

    gd = POOL_GROUP_DIM
    part_rows = tm // OUT_PARTS
    lane = lax.broadcasted_iota(I32, (part_rows, V7X_LANES), 1)
    lane_f = lane.astype(F32)
    neg_inf = jnp.float32(-jnp.inf)
    tri = (lax.broadcasted_iota(I32, (part_rows, part_rows), 1)
           < lax.broadcasted_iota(I32, (part_rows, part_rows), 0)).astype(BF16)

    def mix(r0, halo):
        rows = slice(r0, r0 + part_rows)
        u = u_ref[rows, :]
        pos1 = seq_tile * tm + r0 + lax.broadcasted_iota(I32, (part_rows, gd), 0) + 1
        mixed = []
        for gi, win in enumerate(POOL_WINDOWS):
            cols = slice(gi * gd, (gi + 1) * gd)
            ug = u[:, cols]
            ws = jnp.concatenate([halo[:, cols], ug], axis=0)
            s = 1
            while s < win:
                ws = ws + pltpu.roll(ws, s, 0)
                s *= 2
            count = jnp.minimum(pos1, win).astype(F32)
            pooled = ws[POOL_HALO:, :] / count - ug
            m = jnp.dot(pooled.astype(BF16), wp_ref[gi], preferred_element_type=F32)
            mixed.append(((m + bp_ref[:, cols]) * ps_ref[:, cols]).astype(BF16))
        cat = jnp.concatenate(mixed + [yg_ref[rows, :]], axis=1)
        return jnp.dot(cat, wo_ref[...], preferred_element_type=F32)

    def norms(r0, y):
        rows = slice(r0, r0 + part_rows)
        x1 = _layer_norm(DEEPNORM_ALPHA * x_ref[rows, :] + g1_ref[0] * y) * l1g_ref[...] + l1b_ref[...]
        x1_ref[rows, :] = x1
        h2 = _layer_norm(x1) * (1.0 + sc2_ref[0]) + sh2_ref[0]
        h2_ref[rows, :] = h2
        hb = h2.astype(BF16)
        return hb, (h2 - hb.astype(F32)).astype(BF16)

    def route(h_hi, h_lo):
        ra = jnp.dot(h_hi, wra_ref[...], preferred_element_type=F32)
        rb = jnp.dot(h_lo, wrb_ref[...], preferred_element_type=F32)
        return ra + pltpu.roll(ra, V7X_LANES - N_EXPERTS, 1) + rb + br_ref[...]

    def select(r0, logits, carry):
        rows = slice(r0, r0 + part_rows)
        work = jnp.where(lane < N_EXPERTS, logits, neg_inf)
        vals, idxs = [], []
        for _ in range(TOP_K):
            mx = jnp.max(work, axis=-1, keepdims=True)
            sel = jnp.min(jnp.where(work == mx, lane_f, float(V7X_LANES)), axis=-1,
                          keepdims=True).astype(I32)
            vals.append(mx)
            idxs.append(sel)
            work = jnp.where(lane == sel, neg_inf, work)
        exps = [jnp.exp(v - vals[0]) for v in vals]
        denom = exps[0] + exps[1] + exps[2] + exps[3]

        chosen = jnp.where((lane < N_EXPERTS) & (work == neg_inf), 1.0, 0.0)
        excl = jnp.dot(tri, chosen.astype(BF16), preferred_element_type=F32) + carry
        idx_out = jnp.zeros((part_rows, V7X_LANES), I32)
        tw_out = jnp.zeros((part_rows, V7X_LANES), F32)
        rank_out = jnp.zeros((part_rows, V7X_LANES), F32)
        for kk in range(TOP_K):
            rk = jnp.sum(jnp.where(lane == idxs[kk], excl, 0.0), axis=-1, keepdims=True)
            idx_out = jnp.where(lane == kk, idxs[kk], idx_out)
            tw_out = jnp.where(lane == kk, exps[kk] / denom, tw_out)
            rank_out = jnp.where(lane == kk, rk, rank_out)
        idx_ref[rows, :] = idx_out
        tw_ref[rows, :] = tw_out
        rank_ref[rows, :] = rank_out.astype(I32)
        return carry + jnp.sum(chosen, axis=0, keepdims=True)

    starts = [p * part_rows for p in range(OUT_PARTS)]
    halos = [halo_ref[...]] + [u_ref[r0 - POOL_HALO:r0, :] for r0 in starts[1:]]
    ys = [mix(r0, halo) for r0, halo in zip(starts, halos)]
    carry = carry_ref[0:1, :]
    pending = None
    for r0, y in zip(starts, ys):
        h_hi, h_lo = norms(r0, y)
        if pending is not None:
            carry = select(*pending, carry)
        pending = (r0, route(h_hi, h_lo))
    carry = select(*pending, carry)
    halo_ref[...] = u_ref[tm - POOL_HALO:tm, :]
    carry_ref[...] = jnp.broadcast_to(carry, carry_ref.shape)
    cnt_ref[...] = jnp.broadcast_to(carry, cnt_ref.shape)


def _outproj_call(proj, y_gla, x2, g1, sh2, sc2, wp, bp, ps, wo, l1g, l1b, wra, wrb, br, seq):
    t, d = x2.shape
    tm = OUT_TM
    tps = seq // tm
    bidx = lambda i: (i // tps, 0, 0)
    const2 = lambda i: (0, 0)
    row = lambda i: (i, 0)
    lanes = V7X_LANES
    vmem = (2 * tm * (POOL_WIDTH * 4 + GLA_V_WIDTH * 2 + d * 4 + d * 4 + d * 4 + 3 * lanes * 4)
            + d * d * 2 + 16 * 1024 * 1024)
    return pl.pallas_call(
        functools.partial(_outproj_kernel, tiles_per_seq=tps),
        grid=(t // tm,),
        in_specs=[
            pl.BlockSpec((tm, POOL_WIDTH), row),
            pl.BlockSpec((tm, GLA_V_WIDTH), row),
            pl.BlockSpec((tm, d), row),
            pl.BlockSpec((1, 1, d), bidx),
            pl.BlockSpec((1, 1, d), bidx),
            pl.BlockSpec((1, 1, d), bidx),
            pl.BlockSpec((POOL_GROUPS, POOL_GROUP_DIM, POOL_GROUP_DIM), lambda i: (0, 0, 0)),
            pl.BlockSpec((1, POOL_WIDTH), const2),
            pl.BlockSpec((1, POOL_WIDTH), const2),
            pl.BlockSpec((d, d), const2, pipeline_mode=pl.Buffered(1)),
            pl.BlockSpec((1, d), const2),
            pl.BlockSpec((1, d), const2),
            pl.BlockSpec((d, lanes), const2),
            pl.BlockSpec((d, lanes), const2),
            pl.BlockSpec((1, lanes), const2),
        ],
        out_specs=[
            pl.BlockSpec((tm, d), row),
            pl.BlockSpec((tm, d), row),
            pl.BlockSpec((tm, lanes), row),
            pl.BlockSpec((tm, lanes), row),
            pl.BlockSpec((tm, lanes), row),
            pl.BlockSpec((V7X_SUBLANES, lanes), const2),
        ],
        out_shape=[
            jax.ShapeDtypeStruct((t, d), F32),
            jax.ShapeDtypeStruct((t, d), F32),
            jax.ShapeDtypeStruct((t, lanes), I32),
            jax.ShapeDtypeStruct((t, lanes), F32),
            jax.ShapeDtypeStruct((t, lanes), I32),
            jax.ShapeDtypeStruct((V7X_SUBLANES, lanes), F32),
        ],
        scratch_shapes=[
            pltpu.VMEM((POOL_HALO, POOL_WIDTH), F32),
            pltpu.VMEM((V7X_SUBLANES, lanes), F32),
        ],
        compiler_params=pltpu.CompilerParams(
            dimension_semantics=("arbitrary",), vmem_limit_bytes=_vmem_limit(vmem)),
        name="outproj",
    )(proj, y_gla, x2, g1, sh2, sc2, wp, bp, ps, wo, l1g, l1b, wra, wrb, br)


def _dispatch_kernel(tbl_ref, pos_ref, h_ref, xs_ref, zero_ref, sem, zsem):
    tm = h_ref.shape[0]
    n_experts = tbl_ref.shape[1]
    n_tiles = xs_ref.shape[0] // MOE_TM

    def pad_copy(p):
        return pltpu.make_async_copy(zero_ref.at[pl.ds(0, 1)], xs_ref.at[pl.ds(p, 1)], zsem)

    def tail_copy(tile):
        dst = xs_ref.at[pl.ds(pl.multiple_of(tile * MOE_TM, MOE_TM), MOE_TM)]
        return pltpu.make_async_copy(zero_ref, dst, zsem)

    def for_each_pad(fn):
        def per_expert(e, c):
            cnt = tbl_ref[0, e]
            first = tbl_ref[1, e] + cnt
            n_pad = (cnt + MOE_TM - 1) // MOE_TM * MOE_TM - cnt

            def per_row(r, c2):
                fn(pad_copy(first + r))
                return c2
            lax.fori_loop(0, n_pad, per_row, 0)
            return c
        lax.fori_loop(0, n_experts, per_expert, 0)
        last = n_experts - 1
        tail0 = (tbl_ref[1, last] + tbl_ref[0, last] + MOE_TM - 1) // MOE_TM

        def per_tile(r, c):
            fn(tail_copy(tail0 + r))
            return c
        lax.fori_loop(0, n_tiles - tail0, per_tile, 0)

    @pl.when(pl.program_id(0) == 0)
    def _():
        zero_ref[...] = jnp.zeros_like(zero_ref)
        for_each_pad(lambda cp: cp.start())
        for_each_pad(lambda cp: cp.wait())

    def row_copy(t, p):
        return pltpu.make_async_copy(h_ref.at[pl.ds(t, 1)], xs_ref.at[pl.ds(p, 1)], sem)

    def issue(t, carry):
        for kk in range(TOP_K):
            row_copy(t, pos_ref[0, 0, t * TOP_K + kk]).start(priority=kk % 2)
        return carry

    lax.fori_loop(0, tm, issue, 0, unroll=2)
    for kk in range(TOP_K):
        pltpu.make_async_copy(h_ref, xs_ref.at[pl.ds(0, tm)], sem).wait()


def _dispatch_call(tbl, pos, h2, n_slots):
    t, w = h2.shape
    tm = DISPATCH_TM
    pos3 = pos.reshape(t // tm, 1, tm * TOP_K)
    return pl.pallas_call(
        _dispatch_kernel,
        grid=(t // tm,),
        in_specs=[
            pl.BlockSpec(memory_space=pltpu.SMEM),
            pl.BlockSpec((1, 1, tm * TOP_K), lambda i: (i, 0, 0), memory_space=pltpu.SMEM),
            pl.BlockSpec((tm, w), lambda i: (i, 0)),
        ],
        out_specs=pl.BlockSpec(memory_space=pl.ANY),
        out_shape=jax.ShapeDtypeStruct((n_slots, w), h2.dtype),
        scratch_shapes=[
            pltpu.VMEM((MOE_TM, w), h2.dtype),
            pltpu.SemaphoreType.DMA(()),
            pltpu.SemaphoreType.DMA(()),
        ],
        compiler_params=pltpu.CompilerParams(dimension_semantics=("arbitrary",)),
        name="dispatch",
    )(tbl, pos3, h2)


def _moe_kernel(item_e_ref, item_row_ref, item_n_ref, tail_ref,
                xs_ref, wg_ref, wu_ref, wd_ref, bg_ref, bu_ref, bd_ref,
                ys_ref,
                xbuf, stage, yacc, wgu_bf, wd_bf, sem_in, sem_out):
    del item_e_ref
    i = pl.program_id(0)
    j = pl.program_id(1)
    n_items = pl.num_programs(0)
    n_chunks = pl.num_programs(1)
    nsub = item_n_ref[i]
    row0 = item_row_ref[i]
    tm, tf = MOE_TM, MOE_TF

    def in_copy(r):
        row = jnp.minimum(row0 + r * tm, xs_ref.shape[0] - tm)
        src = xs_ref.at[pl.ds(pl.multiple_of(row, tm), tm)]
        return pltpu.make_async_copy(src, stage.at[r & 1], sem_in.at[r & 1])

    def out_copy(r):
        dst = ys_ref.at[pl.ds(pl.multiple_of(row0 + r * tm, tm), tm)]
        return pltpu.make_async_copy(yacc.at[pl.ds(pl.multiple_of(r * tm, tm), tm)], dst, sem_out.at[r])

    @pl.when((j == 0) & (nsub > 0))
    def _():
        in_copy(0).start()
        in_copy(1).start()

    def chunk(first, last):
        wgu_bf[:, :tf] = wg_ref[0].astype(BF16)
        wgu_bf[:, tf:] = wu_ref[0].astype(BF16)
        wd_bf[...] = wd_ref[0].astype(BF16)

        def activation(r):
            rows = pl.ds(pl.multiple_of(r * tm, tm), tm)
            if first:
                in_copy(r).wait()
                xb = stage[r & 1].astype(BF16)
                xbuf[rows, :] = xb
                in_copy(r + 2).start()
            else:
                xb = xbuf[rows, :]
            gu = jnp.dot(xb, wgu_bf[...], preferred_element_type=F32)
            gate = jnp.minimum(gu[:, :tf] + bg_ref[0], SWIGLU_LIMIT)
            up = jnp.clip(gu[:, tf:] + bu_ref[0], -SWIGLU_LIMIT, SWIGLU_LIMIT)
            return (gate * jax.nn.sigmoid(SWIGLU_ALPHA * gate) * (up + 1.0)).astype(BF16)

        def down(r, act):
            rows = pl.ds(pl.multiple_of(r * tm, tm), tm)
            part = jnp.dot(act, wd_bf[...], preferred_element_type=F32)
            if first:
                yacc[rows, :] = part + bd_ref[0]
            else:
                yacc[rows, :] = yacc[rows, :] + part
            if last:
                out_copy(r).start()

        def step(r, act):
            nxt = activation(r + 1)
            down(r, act)
            return nxt

        def steps_from(r, count, act):
            for c in range(count):
                act = step(r + c, act)
            return act

        n_steps = nsub - 1
        n_full = n_steps // MOE_UNROLL
        act = lax.fori_loop(0, n_full, lambda p, a: steps_from(MOE_UNROLL * p, MOE_UNROLL, a),
                            activation(0))
        done = n_full * MOE_UNROLL
        rem = n_steps - done
        act = lax.cond(rem >= 2, lambda a: steps_from(done, 2, a), lambda a: a, act)
        done = done + jnp.where(rem >= 2, 2, 0)
        act = lax.cond(rem % 2 == 1, lambda a: step(done, a), lambda a: a, act)
        down(nsub - 1, act)

        if first:
            in_copy(0).wait()
            in_copy(1).wait()
        if last:
            def drain(r, c):
                out_copy(r).wait()
                return c
            lax.fori_loop(0, nsub, drain, 0)

    @pl.when((nsub > 0) & (j == 0))
    def _():
        chunk(True, False)

    @pl.when((nsub > 0) & (j > 0) & (j < n_chunks - 1))
    def _():
        chunk(False, False)

    @pl.when((nsub > 0) & (j == n_chunks - 1))
    def _():
        chunk(False, True)

    @pl.when((i == n_items - 1) & (j == n_chunks - 1))
    def _():
        yacc[0:tm, :] = jnp.zeros((tm, D_MODEL), F32)
        tail0 = tail_ref[0]
        n_tail = ys_ref.shape[0] // tm - tail0

        def tail_copy(r):
            dst = ys_ref.at[pl.ds(pl.multiple_of((tail0 + r) * tm, tm), tm)]
            return pltpu.make_async_copy(yacc.at[0:tm], dst, sem_out.at[0])

        def start(r, c):
            tail_copy(r).start()
            return c

        def drain(r, c):
            tail_copy(r).wait()
            return c

        lax.fori_loop(0, n_tail, start, 0)
        lax.fori_loop(0, n_tail, drain, 0)


def _moe_call(item_e, item_row, item_n, tail_tile, xs, wg, wu, wd, bg, bu, bd):
    n_slots, d = xs.shape
    e, _, f = wg.shape
    n_items = item_e.shape[0]
    n_chunks = f // MOE_TF
    rows = MOE_SUB * MOE_TM

    def jeff(i, j, n_ref):
        return jnp.where(n_ref[i] > 0, j, n_chunks - 1)

    vmem = (rows * d * 2 + 2 * MOE_TM * d * 4 + rows * d * 4 + 3 * 2 * d * MOE_TF * 4
            + 3 * d * MOE_TF * 2 + 6 * 1024 * 1024)
    grid_spec = pltpu.PrefetchScalarGridSpec(
        num_scalar_prefetch=4,
        grid=(n_items, n_chunks),
        in_specs=[
            pl.BlockSpec(memory_space=pl.ANY),
            pl.BlockSpec((1, d, MOE_TF), lambda i, j, e_r, r_r, n_r, t_r: (e_r[i], 0, jeff(i, j, n_r))),
            pl.BlockSpec((1, d, MOE_TF), lambda i, j, e_r, r_r, n_r, t_r: (e_r[i], 0, jeff(i, j, n_r))),
            pl.BlockSpec((1, MOE_TF, d), lambda i, j, e_r, r_r, n_r, t_r: (e_r[i], jeff(i, j, n_r), 0)),
            pl.BlockSpec((1, 1, MOE_TF), lambda i, j, e_r, r_r, n_r, t_r: (e_r[i], 0, jeff(i, j, n_r))),
            pl.BlockSpec((1, 1, MOE_TF), lambda i, j, e_r, r_r, n_r, t_r: (e_r[i], 0, jeff(i, j, n_r))),
            pl.BlockSpec((1, 1, d), lambda i, j, e_r, r_r, n_r, t_r: (e_r[i], 0, 0)),
        ],
        out_specs=pl.BlockSpec(memory_space=pl.ANY),
        scratch_shapes=[
            pltpu.VMEM((rows, d), BF16),
            pltpu.VMEM((2, MOE_TM, d), F32),
            pltpu.VMEM((rows, d), F32),
            pltpu.VMEM((d, 2 * MOE_TF), BF16),
            pltpu.VMEM((MOE_TF, d), BF16),
            pltpu.SemaphoreType.DMA((2,)),
            pltpu.SemaphoreType.DMA((MOE_SUB,)),
        ],
    )
    return pl.pallas_call(
        _moe_kernel,
        grid_spec=grid_spec,
        out_shape=jax.ShapeDtypeStruct((n_slots, d), F32),
        compiler_params=pltpu.CompilerParams(
            dimension_semantics=("arbitrary", "arbitrary"), vmem_limit_bytes=_vmem_limit(vmem)),
        name="moe",
    )(item_e, item_row, item_n, tail_tile, xs, wg, wu, wd, bg.reshape(e, 1, f), bu.reshape(e, 1, f),
      bd.reshape(e, 1, d))


def _combine_kernel(pos_ref, posn_ref, ys_ref, x1_ref, tw_ref, g2_ref, l2g_ref, l2b_ref,
                    o_ref, buf, sem):
    i = pl.program_id(0)
    n = pl.num_programs(0)
    tm = x1_ref.shape[0]
    slot = i % 2

    def row_copy(p_ref, t, kk, s):
        p = p_ref[0, 0, t * TOP_K + kk]
        return pltpu.make_async_copy(ys_ref.at[pl.ds(p, 1)], buf.at[s, pl.ds(kk * tm + t, 1)], sem.at[s])

    def issue(p_ref, s):
        def body(t, c):
            for kk in range(TOP_K):
                row_copy(p_ref, t, kk, s).start(priority=kk % 2)
            return c
        lax.fori_loop(0, tm, body, 0, unroll=2)

    @pl.when(i == 0)
    def _():
        issue(pos_ref, 0)

    for s in range(2):
        @pl.when((i + 1 < n) & (slot == 1 - s))
        def _():
            issue(posn_ref, s)

    pltpu.make_async_copy(ys_ref.at[pl.ds(0, TOP_K * tm)], buf.at[slot], sem.at[slot]).wait()

    tw = tw_ref[...]
    y = None
    for kk in range(TOP_K):
        part = tw[:, kk:kk + 1] * buf[slot, kk * tm:(kk + 1) * tm, :]
        y = part if y is None else y + part
    r = DEEPNORM_ALPHA * x1_ref[...] + g2_ref[0] * y
    o_ref[...] = _layer_norm(r) * l2g_ref[...] + l2b_ref[...]


def _combine_call(pos, ys, x1, tw, g2, l2g, l2b, seq):
    t, d = x1.shape
    tm = COMBINE_TM
    n = t // tm
    tps = seq // tm
    pos3 = pos.reshape(n, 1, tm * TOP_K)
    lanes = V7X_LANES
    vmem = 2 * TOP_K * tm * d * 4 + 4 * tm * d * 4 + 8 * 1024 * 1024
    return pl.pallas_call(
        _combine_kernel,
        grid=(n,),
        in_specs=[
            pl.BlockSpec((1, 1, tm * TOP_K), lambda i: (i, 0, 0), memory_space=pltpu.SMEM),
            pl.BlockSpec((1, 1, tm * TOP_K), lambda i: (jnp.minimum(i + 1, n - 1), 0, 0),
                         memory_space=pltpu.SMEM),
            pl.BlockSpec(memory_space=pl.ANY),
            pl.BlockSpec((tm, d), lambda i: (i, 0)),
            pl.BlockSpec((tm, lanes), lambda i: (i, 0)),
            pl.BlockSpec((1, 1, d), lambda i: (i // tps, 0, 0)),
            pl.BlockSpec((1, d), lambda i: (0, 0)),
            pl.BlockSpec((1, d), lambda i: (0, 0)),
        ],
        out_specs=pl.BlockSpec((tm, d), lambda i: (i, 0)),
        out_shape=jax.ShapeDtypeStruct((t, d), F32),
        scratch_shapes=[
            pltpu.VMEM((2, TOP_K * tm, d), F32),
            pltpu.SemaphoreType.DMA((2,)),
        ],
        compiler_params=pltpu.CompilerParams(
            dimension_semantics=("arbitrary",), vmem_limit_bytes=_vmem_limit(vmem)),
        name="combine",
    )(pos3, pos3, ys, x1, tw, g2, l2g, l2b)


def _routing_tables(counts, idx4, rank4, n_items):
    e = counts.shape[0]
    padded = (counts + MOE_TM - 1) // MOE_TM * MOE_TM
    pad_end = jnp.cumsum(padded)
    pad_start = pad_end - padded
    pos = pad_start[idx4] + rank4

    nsub_e = padded // MOE_TM
    items_e = (nsub_e + MOE_SUB - 1) // MOE_SUB
    item_end = jnp.cumsum(items_e)
    item_start = item_end - items_e
    total = item_end[-1]
    w = jnp.arange(n_items, dtype=I32)
    last = jnp.maximum(total - 1, 0)
    wv = jnp.minimum(w, last)
    ew = jnp.minimum(jnp.sum((item_end[None, :] <= wv[:, None]).astype(I32), axis=1), e - 1)
    s = wv - item_start[ew]
    valid = w < total
    item_row = jnp.where(valid, pad_start[ew] + s * (MOE_SUB * MOE_TM), 0).astype(I32)
    item_n = jnp.where(valid, jnp.clip(nsub_e[ew] - s * MOE_SUB, 0, MOE_SUB), 0).astype(I32)
    tail_tile = (pad_end[-1:] // MOE_TM).astype(I32)
    group_tbl = jnp.stack([counts, pad_start]).astype(I32)
    return pos.astype(I32), ew, item_row, item_n, tail_tile, group_tbl


def kernel(x, c, w_ada, b_ada, w_in, w_gk, b_gk, w_pool, b_pool, pool_scale, gla_norm_w, w_out,
           ln1_g, ln1_b, w_router, b_router, w_gate, b_gate, w_up, b_up, w_down, b_down,
           ln2_g, ln2_b):
    batch, seq, d = x.shape
    t = batch * seq
    e = w_router.shape[-1]
    lanes = V7X_LANES
    for l in range(w_ada.shape[0]):
        mod = _mod_call(c, w_ada[l], b_ada[l])
        sh1, sc1, g1, sh2, sc2, g2 = [m.reshape(batch, 1, d) for m in jnp.split(mod, 6, axis=-1)]

        wi = w_in[l]
        lr0 = POOL_WIDTH + 2 * GLA_K_WIDTH + GLA_V_WIDTH
        w_re = jnp.concatenate(
            [wi[:, :lr0], wi[:, lr0 + GLA_GATE_RANK:], wi[:, lr0:lr0 + GLA_GATE_RANK],
             jnp.zeros((d, lanes - GLA_GATE_RANK), wi.dtype)], axis=1).astype(BF16)
        x2 = x.reshape(t, d)
        proj = _inproj_call(x2, sh1, sc1, w_re, seq)

        wgk_pad = jnp.concatenate(
            [w_gk[l], jnp.zeros((lanes - GLA_GATE_RANK, GLA_K_WIDTH), w_gk.dtype)], axis=0).astype(BF16)
        y_gla = _gla_call(proj, wgk_pad, b_gk[l].reshape(1, -1), gla_norm_w[l].reshape(1, -1), batch, seq)

        wr = w_router[l]
        wr_hi = wr.astype(BF16)
        wr_lo = (wr - wr_hi.astype(F32)).astype(BF16)
        wra = jnp.concatenate([wr_hi, wr_lo, jnp.zeros((d, lanes - 2 * e), BF16)], axis=1)
        wrb = jnp.concatenate([wr_hi, jnp.zeros((d, lanes - e), BF16)], axis=1)
        br = jnp.concatenate([b_router[l], jnp.zeros((lanes - e,), F32)]).reshape(1, lanes)
        x1, h2, idx, tw, rank, cnt = _outproj_call(
            proj, y_gla, x2, g1, sh2, sc2,
            w_pool[l].astype(BF16), b_pool[l].reshape(1, -1), pool_scale[l].reshape(1, -1),
            w_out[l].astype(BF16), ln1_g[l].reshape(1, -1), ln1_b[l].reshape(1, -1),
            wra, wrb, br, seq)

        n_slots = t * TOP_K + e * MOE_TM
        n_items = e + n_slots // (MOE_SUB * MOE_TM)
        counts = cnt[0, :e].astype(I32)
        pos, item_e, item_row, item_n, tail_tile, group_tbl = _routing_tables(
            counts, idx[:, :TOP_K], rank[:, :TOP_K], n_items)

        xs = _dispatch_call(group_tbl, pos, h2, n_slots)
        ys = _moe_call(item_e, item_row, item_n, tail_tile, xs, w_gate[l], w_up[l], w_down[l],
                       b_gate[l], b_up[l], b_down[l])
        out = _combine_call(pos, ys, x1, tw, g2, ln2_g[l].reshape(1, -1), ln2_b[l].reshape(1, -1), seq)
        x = out.reshape(batch, seq, d)
    return x
```
